```python
import jax
import jax.numpy as jnp
from jax import lax
import numpy as np

D_MODEL = 2048
BATCH = 8
SEQ = 4096
DEPTH = 2

GRID_W = 64
CTX_LEN = 256
HEAD_DIM = 128
N_BRANCH = 4
BRANCH_W = D_MODEL // 4
POOL_WINDOWS = (2, 4, 8, 16)
POOL_GROUP = BRANCH_W // len(POOL_WINDOWS)
RWKV_HEAD = 64
RWKV_HEADS = BRANCH_W // RWKV_HEAD
DECAY_LORA = 96
ICLR_LORA = 96
GATE_LORA = 256
RWKV_SHIFT_W = 3 * BRANCH_W + 2 * DECAY_LORA + 2 * ICLR_LORA + GATE_LORA
WIN = 128
BLOCK = 128
C_QH = 4
C_KVH = 2
D_QH = 4
D_KVH = 2
ATT_W_C = (C_QH + 2 * C_KVH) * HEAD_DIM
ATT_W_D = (D_QH + 2 * D_KVH) * HEAD_DIM
IN_W = BRANCH_W + RWKV_SHIFT_W + ATT_W_C + ATT_W_D + N_BRANCH * D_MODEL
D_FF = 256 * ((8 * D_MODEL // 3 + 255) // 256)
N_MOD = 9
ROPE_THETA = 10000.0
DEEPNORM_ALPHA = (2 * DEPTH) ** 0.25
DEEPNORM_BETA = (8 * DEPTH) ** -0.25
LN_EPS = 1e-6
RMS_EPS = 1e-6
GN_EPS = 64e-5
NEG_INF = -1e30
F32 = jnp.float32

kernel_name = 'hybrid_pool_rwkv7_swa_axialgqa_dit_trunk'


def _layer_norm(z, g, b):
    zf = z.astype(F32)
    mu = jnp.mean(zf, -1, keepdims=True)
    var = jnp.mean(jnp.square(zf - mu), -1, keepdims=True)
    return ((zf - mu) * lax.rsqrt(var + LN_EPS) * g + b).astype(z.dtype)


def _post_norm(x, y, gate, g, b):
    return _layer_norm(DEEPNORM_ALPHA * x + gate * y, g, b)


def _modulate(x, shift, scale):
    return x * (1.0 + scale) + shift


def _swiglu(h, wi, wo):
    gate, up = jnp.split(h @ wi, 2, axis=-1)
    return (jax.nn.silu(gate) * up) @ wo


def _rms_norm(u, g):
    uf = u.astype(F32)
    return (uf * lax.rsqrt(jnp.mean(uf * uf, -1, keepdims=True) + RMS_EPS) * g).astype(u.dtype)


def _axial_rope(rows):
    n = rows * GRID_W
    row = jnp.repeat(jnp.arange(rows), GRID_W).astype(F32)
    col = (jnp.arange(n) % GRID_W).astype(F32)
    n_freq = HEAD_DIM // 4
    inv = ROPE_THETA ** (-jnp.arange(n_freq, dtype=F32) / n_freq)
    ang = jnp.stack([row[:, None] * inv, col[:, None] * inv], axis=1)
    return jnp.cos(ang), jnp.sin(ang)


def _rope(x, cos, sin):
    B, S, H, _ = x.shape
    xr = x.astype(F32).reshape(B, S, H, 2, 2, HEAD_DIM // 4)
    x1, x2 = xr[..., 0, :], xr[..., 1, :]
    c = cos[None, :, None]
    s = sin[None, :, None]
    out = jnp.stack([x1 * c - x2 * s, x2 * c + x1 * s], axis=-2)
    return out.reshape(x.shape).astype(x.dtype)


def _split_in(z, b_gate):
    o1 = BRANCH_W
    o2 = o1 + RWKV_SHIFT_W
    o3 = o2 + ATT_W_C
    o4 = o3 + ATT_W_D
    gates = jax.nn.sigmoid((z[..., o4:] + b_gate).astype(F32)).astype(z.dtype)
    return z[..., :o1], z[..., o1:o2], z[..., o2:o3], z[..., o3:o4], gates


def _qkv(z, hq, hk):
    B, T, _ = z.shape
    q = z[..., :hq * HEAD_DIM].reshape(B, T, hq, HEAD_DIM)
    k = z[..., hq * HEAD_DIM:(hq + hk) * HEAD_DIM].reshape(B, T, hk, HEAD_DIM)
    v = z[..., (hq + hk) * HEAD_DIM:].reshape(B, T, hk, HEAD_DIM)
    return q, k, v


def _pool_branch(u, pool_w, pool_scale):
    B, T, _ = u.shape
    ug = u.astype(F32).reshape(B, T, len(POOL_WINDOWS), POOL_GROUP)
    cs = jnp.concatenate([jnp.zeros((B, 1) + ug.shape[2:], F32), jnp.cumsum(ug, axis=1)], axis=1)
    t = jnp.arange(T)
    means = []
    for gi, win in enumerate(POOL_WINDOWS):
        lo = jnp.clip(t - win // 2, 0, T)
        hi = jnp.clip(t + win // 2, 0, T)
        means.append((cs[:, hi, gi] - cs[:, lo, gi]) / (hi - lo).astype(F32)[None, :, None])
    pooled = jnp.stack(means, axis=2) - ug
    y = jnp.einsum('btgc,gcd->btgd', pooled, pool_w.astype(F32)).reshape(B, T, BRANCH_W)
    return (y * pool_scale).astype(u.dtype)


def _centred_shift(u, mu):
    zero = jnp.zeros_like(u[:, :1])
    prev = jnp.concatenate([zero, u[:, :-1]], axis=1)
    nxt = jnp.concatenate([u[:, 1:], zero], axis=1)
    return u + (0.5 * (prev + nxt) - u) * mu


def _heads64(u):
    return u.reshape(u.shape[:-1] + (RWKV_HEADS, RWKV_HEAD))


def _rwkv_prep(zr, p):
    B, T, _ = zr.shape
    zr = _centred_shift(zr.astype(F32), p['rwkv_mu'])
    o1, o2, o3 = BRANCH_W, 2 * BRANCH_W, 3 * BRANCH_W
    o4 = o3 + 2 * DECAY_LORA
    o5 = o4 + 2 * ICLR_LORA
    r, k, v = zr[..., :o1], zr[..., o1:o2], zr[..., o2:o3]
    wd = zr[..., o3:o4].reshape(B, T, 2, DECAY_LORA)
    ad = zr[..., o4:o5].reshape(B, T, 2, ICLR_LORA)
    gd = zr[..., o5:]
    z_w = p['rwkv_w0'] + jnp.einsum('btdr,drc->btdc', jnp.tanh(wd), p['rwkv_w2'])
    decay = jnp.exp(-jnp.exp(-jax.nn.softplus(-z_w) - 0.5))
    a = jax.nn.sigmoid(p['rwkv_a0'] + jnp.einsum('btdr,drc->btdc', ad, p['rwkv_a2']))
    g = jax.nn.sigmoid(gd) @ p['rwkv_g2']
    kk = _heads64(k * p['rwkv_kk'])
    kk = kk * lax.rsqrt(jnp.sum(kk * kk, -1, keepdims=True) + 1e-12)
    a_h = _heads64(a)
    k_rep = _heads64(k)[:, :, None] * (1.0 + (a_h - 1.0) * _heads64(p['rwkv_ka']))
    b = kk[:, :, None] * a_h
    return _heads64(r), _heads64(decay), k_rep, _heads64(v), kk, b, g


def _dirs_shared(u):
    return jnp.moveaxis(jnp.stack([u, u[:, ::-1]], 0), 2, 0)


def _dirs_split(u):
    return jnp.moveaxis(jnp.stack([u[:, :, 0], u[:, ::-1, 1]], 0), 2, 0)


def _rwkv_scan(S0, r, w, k, v, kk, b):
    def step(S, inp):
        r_t, w_t, k_t, v_t, kk_t, b_t = inp
        sk = jnp.einsum('dbhij,dbhj->dbhi', S, kk_t)
        S = S * w_t[..., None, :] - sk[..., :, None] * b_t[..., None, :] + v_t[..., :, None] * k_t[..., None, :]
        return S, jnp.einsum('dbhij,dbhj->dbhi', S, r_t)
    S, o = lax.scan(step, S0, (r, w, k, v, kk, b))
    return S, jnp.moveaxis(o, 0, 2)


def _rwkv_scan_inputs(prep):
    r, w, k_rep, v, kk, b, _ = prep
    return (_dirs_shared(r), _dirs_split(w), _dirs_split(k_rep), _dirs_shared(v), _dirs_shared(kk), _dirs_split(b))


def _rwkv_out(o, prep, p):
    r, _, k_rep, v, _, _, g = prep
    B, T = r.shape[:2]
    s = o[0] + o[1][:, ::-1]
    mu = jnp.mean(s, -1, keepdims=True)
    var = jnp.mean(jnp.square(s - mu), -1, keepdims=True)
    y = ((s - mu) * lax.rsqrt(var + GN_EPS)).reshape(B, T, BRANCH_W) * p['rwkv_gn_g'] + p['rwkv_gn_b']
    rk = _heads64(p['rwkv_rk'])
    bonus = jnp.sum(r[:, :, None] * k_rep * rk, axis=(2, 4))[..., None] * v
    return (y + bonus.reshape(B, T, BRANCH_W)) * g


def _rwkv_branch(zr_x, zr_c, p, ctx_out):
    B = zr_x.shape[0]
    prep_c = _rwkv_prep(zr_c, p)
    prep_x = _rwkv_prep(zr_x, p)
    S0 = jnp.zeros((2, B, RWKV_HEADS, RWKV_HEAD, RWKV_HEAD), F32)
    S_ctx, o_c = _rwkv_scan(S0, *_rwkv_scan_inputs(prep_c))
    _, o_x = _rwkv_scan(S_ctx, *_rwkv_scan_inputs(prep_x))
    y_x = _rwkv_out(o_x, prep_x, p).astype(zr_x.dtype)
    y_c = _rwkv_out(o_c, prep_c, p).astype(zr_c.dtype) if ctx_out else None
    return y_x, y_c


def _window_sink_attn(q, k, v, kc, vc, sink):
    B, S, Hq, Dh = q.shape
    Hk = k.shape[2]
    G = Hq // Hk
    nb = S // BLOCK
    L = kc.shape[1]
    scale = Dh ** -0.5
    qb = q.reshape(B, nb, BLOCK, Hk, G, Dh)
    pad = jnp.zeros((B, BLOCK, Hk, Dh), k.dtype)

    def band(t):
        tb = jnp.concatenate([pad, t, pad], axis=1).reshape(B, nb + 2, BLOCK, Hk, Dh)
        return jnp.concatenate([tb[:, :-2], tb[:, 1:-1], tb[:, 2:]], axis=2)

    kw, vw = band(k), band(v)
    qpos = (jnp.arange(nb) * BLOCK)[:, None, None] + jnp.arange(BLOCK)[None, :, None]
    kpos = (jnp.arange(nb) * BLOCK - BLOCK)[:, None, None] + jnp.arange(3 * BLOCK)[None, None, :]
    allowed = (jnp.abs(qpos - kpos) <= WIN) & (kpos >= 0) & (kpos < S)
    s_loc = jnp.einsum('bnqhgd,bnkhd->bnhgqk', qb, kw).astype(F32) * scale
    s_loc = jnp.where(allowed[None, :, None, None], s_loc, NEG_INF)
    s_ctx = jnp.einsum('bnqhgd,bchd->bnhgqc', qb, kc).astype(F32) * scale
    s_sink = jnp.broadcast_to(sink.astype(F32).reshape(1, 1, Hk, G, 1, 1), s_ctx.shape[:-1] + (1,))
    p = jax.nn.softmax(jnp.concatenate([s_ctx, s_loc, s_sink], axis=-1), axis=-1).astype(v.dtype)
    o = (jnp.einsum('bnhgqc,bchd->bnqhgd', p[..., :L], vc)
         + jnp.einsum('bnhgqk,bnkhd->bnqhgd', p[..., L:L + 3 * BLOCK], vw))
    return o.reshape(B, S, Hq * Dh)


def _gqa_full(q, k, v, sink):
    B, T, Hq, Dh = q.shape
    Hk = k.shape[2]
    G = Hq // Hk
    s = jnp.einsum('bqhgd,bkhd->bhgqk', q.reshape(B, T, Hk, G, Dh), k).astype(F32) * Dh ** -0.5
    if sink is not None:
        s = jnp.concatenate([s, jnp.broadcast_to(sink.astype(F32).reshape(Hk, G, 1, 1), s.shape[:-1] + (1,))], axis=-1)
    p = jax.nn.softmax(s, axis=-1)[..., :k.shape[1]].astype(v.dtype)
    return jnp.einsum('bhgqk,bkhd->bqhgd', p, v).reshape(B, T, Hq * Dh)


def _dense_block_attn(q, k, v):
    B, S, Hq, Dh = q.shape
    Hk = k.shape[2]
    G = Hq // Hk
    nb = S // BLOCK
    scale = Dh ** -0.5
    qb = jnp.moveaxis(q.reshape(B, nb, BLOCK, Hk, G, Dh), 1, 0)

    def block(qi):
        s = jnp.einsum('bqhgd,bkhd->bhgqk', qi, k).astype(F32) * scale
        p = jax.nn.softmax(s, axis=-1).astype(v.dtype)
        return jnp.einsum('bhgqk,bkhd->bqhgd', p, v)

    o = lax.map(block, qb)
    return jnp.moveaxis(o, 0, 1).reshape(B, S, Hq * Dh)


def _merge(branches, gates, w_up, w_out):
    B, T, _ = gates.shape
    g = gates.reshape(B, T, N_BRANCH, D_MODEL)
    acc = g[:, :, 0] * (branches[0] @ w_up[0])
    for i in range(1, N_BRANCH):
        acc = acc + g[:, :, i] * (branches[i] @ w_up[i])
    return acc @ w_out


def _mixer(hx, hc, p, cos, sin, ctx_out):
    zx = hx @ p['w_in']
    zc = hc @ p['w_in']
    pool_x, rwkv_x, win_x, dense_x, gate_x = _split_in(zx, p['b_gate'])
    pool_c, rwkv_c, win_c, dense_c, gate_c = _split_in(zc, p['b_gate'])
    ya_x = _pool_branch(pool_x, p['pool_w'], p['pool_scale'])
    yb_x, yb_c = _rwkv_branch(rwkv_x, rwkv_c, p, ctx_out)
    qx, kx, vx = _qkv(win_x, C_QH, C_KVH)
    qc, kc, vc = _qkv(win_c, C_QH, C_KVH)
    yc_x = _window_sink_attn(_rope(qx, cos, sin), _rope(kx, cos, sin), vx, kc, vc, p['c_sink'])
    dqx, dkx, dvx = _qkv(dense_x, D_QH, D_KVH)
    dqc, dkc, dvc = _qkv(dense_c, D_QH, D_KVH)
    dqx = _rope(_rms_norm(dqx, p['d_qnorm']), cos, sin)
    dkx = _rope(_rms_norm(dkx, p['d_knorm']), cos, sin)
    dqc = _rms_norm(dqc, p['d_qnorm'])
    dkc = _rms_norm(dkc, p['d_knorm'])
    yd_x = _dense_block_attn(dqx, jnp.concatenate([dkc, dkx], axis=1), jnp.concatenate([dvc, dvx], axis=1))
    out_x = _merge((ya_x, yb_x, yc_x, yd_x), gate_x, p['w_up'], p['w_out'])
    if not ctx_out:
        return out_x, None
    ya_c = _pool_branch(pool_c, p['pool_w'], p['pool_scale'])
    yc_c = _gqa_full(qc, kc, vc, p['c_sink'])
    yd_c = _gqa_full(dqc, dkc, dvc, None)
    out_c = _merge((ya_c, yb_c, yc_c, yd_c), gate_c, p['w_up'], p['w_out'])
    return out_x, out_c


def setup_inputs(seed: int = 0) -> dict:
    key = jax.random.key(seed)
    ks = iter(jax.random.split(key, 40))

    def nrm(shape, scale):
        return scale * jax.random.normal(next(ks), shape, F32)

    def unif(shape, lo, hi):
        return jax.random.uniform(next(ks), shape, F32, lo, hi)

    L, D, C = DEPTH, D_MODEL, BRANCH_W
    return {
        'x': nrm((BATCH, SEQ, D), 1.0),
        'c': nrm((BATCH, D), 1.0),
        'ctx': nrm((BATCH, CTX_LEN, D), 1.0),
        'c_ctx': nrm((D,), 1.0),
        'w_mod': nrm((L, D, N_MOD * D), 0.5 * D ** -0.5),
        'b_mod': nrm((L, N_MOD * D), 0.02),
        'ln_g': 1.0 + nrm((L, 3, D), 0.02),
        'ln_b': nrm((L, 3, D), 0.02),
        'ffn1_wi': nrm((L, D, 2 * D_FF), D ** -0.5),
        'ffn1_wo': nrm((L, D_FF, D), DEEPNORM_BETA * D_FF ** -0.5),
        'ffn2_wi': nrm((L, D, 2 * D_FF), D ** -0.5),
        'ffn2_wo': nrm((L, D_FF, D), DEEPNORM_BETA * D_FF ** -0.5),
        'w_in': nrm((L, D, IN_W), D ** -0.5),
        'b_gate': nrm((L, N_BRANCH * D), 0.02),
        'pool_w': nrm((L, len(POOL_WINDOWS), POOL_GROUP, POOL_GROUP), POOL_GROUP ** -0.5),
        'pool_scale': 1.0 + nrm((L, C), 0.02),
        'rwkv_mu': unif((L, RWKV_SHIFT_W), 0.0, 1.0),
        'rwkv_w0': unif((L, 2, C), -6.0, 1.0),
        'rwkv_w2': nrm((L, 2, DECAY_LORA, C), 0.1 * DECAY_LORA ** -0.5),
        'rwkv_a0': nrm((L, 2, C), 0.5),
        'rwkv_a2': nrm((L, 2, ICLR_LORA, C), 0.5 * ICLR_LORA ** -0.5),
        'rwkv_g2': nrm((L, GATE_LORA, C), GATE_LORA ** -0.5),
        'rwkv_kk': 0.85 + nrm((L, C), 0.05),
        'rwkv_ka': 1.0 + nrm((L, C), 0.05),
        'rwkv_rk': nrm((L, C), 0.1),
        'rwkv_gn_g': 1.0 + nrm((L, C), 0.02),
        'rwkv_gn_b': nrm((L, C), 0.02),
        'c_sink': nrm((L, C_QH), 1.0),
        'd_qnorm': 1.0 + nrm((L, HEAD_DIM), 0.02),
        'd_knorm': 1.0 + nrm((L, HEAD_DIM), 0.02),
        'w_up': nrm((L, N_BRANCH, C, D), C ** -0.5),
        'w_out': nrm((L, D, D), DEEPNORM_BETA * D ** -0.5),
    }


def reference(x, c, ctx, c_ctx, w_mod, b_mod, ln_g, ln_b, ffn1_wi, ffn1_wo, ffn2_wi, ffn2_wo,
              w_in, b_gate, pool_w, pool_scale, rwkv_mu, rwkv_w0, rwkv_w2, rwkv_a0, rwkv_a2, rwkv_g2,
              rwkv_kk, rwkv_ka, rwkv_rk, rwkv_gn_g, rwkv_gn_b, c_sink, d_qnorm, d_knorm, w_up, w_out):
    B, n, _ = x.shape
    rows = n // GRID_W
    cos, sin = _axial_rope(rows)
    sc = jax.nn.silu(c)
    scc = jax.nn.silu(c_ctx)
    xc = ctx
    for l in range(DEPTH):
        last = l == DEPTH - 1
        p = {'w_in': w_in[l], 'b_gate': b_gate[l], 'pool_w': pool_w[l], 'pool_scale': pool_scale[l],
             'rwkv_mu': rwkv_mu[l], 'rwkv_w0': rwkv_w0[l], 'rwkv_w2': rwkv_w2[l], 'rwkv_a0': rwkv_a0[l],
             'rwkv_a2': rwkv_a2[l], 'rwkv_g2': rwkv_g2[l], 'rwkv_kk': rwkv_kk[l], 'rwkv_ka': rwkv_ka[l],
             'rwkv_rk': rwkv_rk[l], 'rwkv_gn_g': rwkv_gn_g[l], 'rwkv_gn_b': rwkv_gn_b[l],
             'c_sink': c_sink[l], 'd_qnorm': d_qnorm[l], 'd_knorm': d_knorm[l],
             'w_up': w_up[l], 'w_out': w_out[l]}
        mod = (sc @ w_mod[l] + b_mod[l]).reshape(B, N_MOD, 1, D_MODEL)
        modc = (scc @ w_mod[l] + b_mod[l]).reshape(N_MOD, D_MODEL)
        x = _post_norm(x, 0.5 * _swiglu(_modulate(x, mod[:, 0], mod[:, 1]), ffn1_wi[l], ffn1_wo[l]),
                       mod[:, 2], ln_g[l, 0], ln_b[l, 0])
        xc = _post_norm(xc, 0.5 * _swiglu(_modulate(xc, modc[0], modc[1]), ffn1_wi[l], ffn1_wo[l]),
                        modc[2], ln_g[l, 0], ln_b[l, 0])
        yx, yc = _mixer(_modulate(x, mod[:, 3], mod[:, 4]), _modulate(xc, modc[3], modc[4]),
                        p, cos, sin, not last)
        x = _post_norm(x, yx, mod[:, 5], ln_g[l, 1], ln_b[l, 1])
        x = _post_norm(x, 0.5 * _swiglu(_modulate(x, mod[:, 6], mod[:, 7]), ffn2_wi[l], ffn2_wo[l]),
                       mod[:, 8], ln_g[l, 2], ln_b[l, 2])
        if not last:
            xc = _post_norm(xc, yc, modc[5], ln_g[l, 1], ln_b[l, 1])
            xc = _post_norm(xc, 0.5 * _swiglu(_modulate(xc, modc[6], modc[7]), ffn2_wi[l], ffn2_wo[l]),
                            modc[8], ln_g[l, 2], ln_b[l, 2])
    return x
```

```python
import functools
import math

import jax
import jax.numpy as jnp
from jax import lax
from jax.experimental import pallas as pl
from jax.experimental.pallas import tpu as pltpu

F32 = jnp.float32
BF16 = jnp.bfloat16

GRID_W = 64
HEAD_DIM = 128
N_BRANCH = 4
BRANCH_W = 512
POOL_WINDOWS = (2, 4, 8, 16)
POOL_GROUP = BRANCH_W // len(POOL_WINDOWS)
RWKV_HEAD = 64
RWKV_HEADS = BRANCH_W // RWKV_HEAD
DECAY_LORA = 96
ICLR_LORA = 96
GATE_LORA = 256
LORA_PAD = 128
RWKV_W = 3 * BRANCH_W + 4 * LORA_PAD + GATE_LORA
WIN = 128
BLOCK = 128
Q_HEADS = 4
KV_HEADS = 2
ATT_W = (Q_HEADS + 2 * KV_HEADS) * HEAD_DIM
N_MOD = 9
DEPTH = 2
ROPE_THETA = 10000.0
DEEPNORM_ALPHA = (2 * DEPTH) ** 0.25
LN_EPS = 1e-6
RMS_EPS = 1e-6
GN_EPS = 64e-5
NEG_INF = -1e30
DECAY_SCALE = math.exp(-0.5)

V7X_VMEM_BYTES = 64 * 1024 * 1024
VMEM_LIMIT = V7X_VMEM_BYTES - 8 * 1024 * 1024
SCAN_CHUNK = 64

_NT = (((1,), (1,)), ((), ()))
_TN = (((0,), (0,)), ((), ()))


def _params(*sem):
    return pltpu.CompilerParams(dimension_semantics=sem, vmem_limit_bytes=VMEM_LIMIT)


def _dot(a, b, dims=None):
    if dims is None:
        return jnp.dot(a, b, preferred_element_type=F32)
    return lax.dot_general(a, b, dims, preferred_element_type=F32)


def _split2(a):
    hi = a.astype(BF16)
    lo = (a - hi.astype(F32)).astype(BF16)
    return hi, lo


def _dot3(a, b, dims=None):
    ah, al = _split2(a)
    bh, bl = _split2(b)
    return _dot(ah, bh, dims) + (_dot(ah, bl, dims) + _dot(al, bh, dims))


def _dot_exact_lhs(a_bf16, b):
    b0 = b.astype(BF16)
    r1 = b - b0.astype(F32)
    b1 = r1.astype(BF16)
    b2 = (r1 - b1.astype(F32)).astype(BF16)
    return _dot(a_bf16, b0) + (_dot(a_bf16, b1) + _dot(a_bf16, b2))


def _dot_exact_rhs(a, b_bf16):
    a0 = a.astype(BF16)
    r1 = a - a0.astype(F32)
    a1 = r1.astype(BF16)
    a2 = (r1 - a1.astype(F32)).astype(BF16)
    return _dot(a0, b_bf16) + (_dot(a1, b_bf16) + _dot(a2, b_bf16))


def _layer_norm(y, g, b):
    mu = jnp.mean(y, axis=-1, keepdims=True)
    yc = y - mu
    var = jnp.mean(yc * yc, axis=-1, keepdims=True)
    return yc * lax.rsqrt(var + LN_EPS) * g + b


def _row_tile(n, want):
    t = min(want, n)
    assert n % t == 0, (n, t)
    return t


def _mod_kernel(c_ref, w_ref, b_ref, o_ref):
    c = c_ref[...]
    s = c * jax.nn.sigmoid(c)
    o_ref[...] = _dot(s.astype(BF16), w_ref[...].astype(BF16)) + b_ref[...]


def _mod_call(c_all, w, b):
    rows, d = c_all.shape
    nout = w.shape[1]
    tn = _row_tile(nout, min(d, 1024))
    return pl.pallas_call(
        _mod_kernel,
        out_shape=jax.ShapeDtypeStruct((rows, nout), F32),
        grid=(nout // tn,),
        in_specs=[pl.BlockSpec((rows, d), lambda j: (0, 0)),
                  pl.BlockSpec((d, tn), lambda j: (0, j)),
                  pl.BlockSpec((1, tn), lambda j: (0, j))],
        out_specs=pl.BlockSpec((rows, tn), lambda j: (0, j)),
        compiler_params=_params("arbitrary"),
        name="mod",
    )(c_all, w, b.reshape(1, nout))


def _mod_spec(d, row_of_tile, k):
    return pl.BlockSpec((1, 1, d), lambda i, j: (row_of_tile(i) * N_MOD + k, 0, 0))


def _ffn_kernel(x_ref, sh_ref, sc_ref, gt_ref, wg_ref, wu_ref, wo_ref, lg_ref, lb_ref,
                o_ref, xm_ref, acc_ref):
    j = pl.program_id(1)

    @pl.when(j == 0)
    def _():
        xm_ref[...] = (x_ref[...] * (1.0 + sc_ref[0]) + sh_ref[0]).astype(BF16)
        acc_ref[...] = jnp.zeros_like(acc_ref)

    xm = xm_ref[...]
    hg = _dot(xm, wg_ref[...])
    hu = _dot(xm, wu_ref[...])
    a = (hg * jax.nn.sigmoid(hg) * hu).astype(BF16)
    acc_ref[...] += _dot(a, wo_ref[...])

    @pl.when(j == pl.num_programs(1) - 1)
    def _():
        y = DEEPNORM_ALPHA * x_ref[...] + gt_ref[0] * (0.5 * acc_ref[...])
        o_ref[...] = _layer_norm(y, lg_ref[...], lb_ref[...])


def _ffn_call(x, mod_rows, row_of_tile_fn, k0, wi, wo, lg, lb, tm_want=512, tf_want=512):
    n, d = x.shape
    dff = wo.shape[0]
    tm = _row_tile(n, tm_want)
    tf = _row_tile(dff, tf_want)
    nj = dff // tf
    rot = functools.partial(row_of_tile_fn, tm)
    return pl.pallas_call(
        _ffn_kernel,
        out_shape=jax.ShapeDtypeStruct((n, d), F32),
        grid=(n // tm, nj),
        in_specs=[pl.BlockSpec((tm, d), lambda i, j: (i, 0)),
                  _mod_spec(d, rot, k0), _mod_spec(d, rot, k0 + 1), _mod_spec(d, rot, k0 + 2),
                  pl.BlockSpec((d, tf), lambda i, j: (0, j)),
                  pl.BlockSpec((d, tf), lambda i, j: (0, j + nj)),
                  pl.BlockSpec((tf, d), lambda i, j: (j, 0)),
                  pl.BlockSpec((1, d), lambda i, j: (0, 0)),
                  pl.BlockSpec((1, d), lambda i, j: (0, 0))],
        out_specs=pl.BlockSpec((tm, d), lambda i, j: (i, 0)),
        scratch_shapes=[pltpu.VMEM((tm, d), BF16), pltpu.VMEM((tm, d), F32)],
        compiler_params=_params("parallel", "arbitrary"),
        name="ffn",
    )(x, mod_rows, mod_rows, mod_rows, wi, wi, wo, lg.reshape(1, d), lb.reshape(1, d))


def _proj_kernel(x_ref, sh_ref, sc_ref, w_ref, *rest, gate):
    if gate:
        b_ref, o_ref, xm_ref = rest
    else:
        o_ref, xm_ref = rest

    @pl.when(pl.program_id(1) == 0)
    def _():
        xm_ref[...] = (x_ref[...] * (1.0 + sc_ref[0]) + sh_ref[0]).astype(BF16)

    z = _dot(xm_ref[...], w_ref[...])
    if gate:
        z = jax.nn.sigmoid(z + b_ref[...])
    o_ref[...] = z.astype(o_ref.dtype)


def _proj_call(x, mod_rows, row_of_tile_fn, k0, w, bias=None, tm_want=512, tn_want=1024):
    n, d = x.shape
    nout = w.shape[1]
    tm = _row_tile(n, tm_want)
    tn = nout if nout % tn_want else tn_want
    rot = functools.partial(row_of_tile_fn, tm)
    in_specs = [pl.BlockSpec((tm, d), lambda i, j: (i, 0)),
                _mod_spec(d, rot, k0), _mod_spec(d, rot, k0 + 1),
                pl.BlockSpec((d, tn), lambda i, j: (0, j))]
    args = [x, mod_rows, mod_rows, w]
    if bias is not None:
        in_specs.append(pl.BlockSpec((1, tn), lambda i, j: (0, j)))
        args.append(bias.reshape(1, nout))
    return pl.pallas_call(
        functools.partial(_proj_kernel, gate=bias is not None),
        out_shape=jax.ShapeDtypeStruct((n, nout), F32),
        grid=(n // tm, nout // tn),
        in_specs=in_specs,
        out_specs=pl.BlockSpec((tm, tn), lambda i, j: (i, j)),
        scratch_shapes=[pltpu.VMEM((tm, d), BF16)],
        compiler_params=_params("parallel", "arbitrary"),
        name="proj",
    )(*args)


def _pool_kernel(z_ref, w_ref, s_ref, o_ref):
    g = pl.program_id(1)
    u = z_ref[...]
    t_len = u.shape[0]
    t = lax.broadcasted_iota(jnp.int32, u.shape, 0)
    for gi, win in enumerate(POOL_WINDOWS):
        @pl.when(g == gi)
        def _(win=win):
            half = win // 2
            acc = u
            for dlt in range(-half, half):
                if dlt == 0:
                    continue
                shifted = pltpu.roll(u, (-dlt) % t_len, 0)
                ok = (t + dlt >= 0) & (t + dlt < t_len)
                acc = acc + jnp.where(ok, shifted, 0.0)
            cnt = (jnp.minimum(t + half, t_len) - jnp.maximum(t - half, 0)).astype(F32)
            pooled = acc / cnt - u
            y = _dot(pooled.astype(BF16), w_ref[0]) * s_ref[...]
            o_ref[...] = y.astype(o_ref.dtype)


def _pool_call(z, nseq, w_bf16, scale):
    n, c = z.shape
    t_len = n // nseq
    ng = len(POOL_WINDOWS)
    return pl.pallas_call(
        _pool_kernel,
        out_shape=jax.ShapeDtypeStruct((n, c), BF16),
        grid=(nseq, ng),
        in_specs=[pl.BlockSpec((t_len, POOL_GROUP), lambda b, g: (b, g)),
                  pl.BlockSpec((1, POOL_GROUP, POOL_GROUP), lambda b, g: (g, 0, 0)),
                  pl.BlockSpec((1, POOL_GROUP), lambda b, g: (0, g))],
        out_specs=pl.BlockSpec((t_len, POOL_GROUP), lambda b, g: (b, g)),
        compiler_params=_params("parallel", "arbitrary"),
        name="pool",
    )(z, w_bf16, scale.reshape(1, c))


def _rwkv_prep_kernel(z_ref, zp_ref, zn_ref, mu_ref, w0_ref, w2_ref, a0_ref, a2_ref, g2_ref,
                      kkp_ref, ka_ref, bd_ref,
                      r_ref, v_ref, kk_ref, g_ref, k_ref, b_ref, lw_ref, *, tiles_per_seq):
    i = pl.program_id(0)
    cur = z_ref[...]
    tm = cur.shape[0]
    c = BRANCH_W
    row = lax.broadcasted_iota(jnp.int32, cur.shape, 0)
    pos = i % tiles_per_seq
    prev_edge = jnp.where(pos == 0, 0.0, zp_ref[7:8, :])
    next_edge = jnp.where(pos == tiles_per_seq - 1, 0.0, zn_ref[0:1, :])
    prev = jnp.where(row == 0, prev_edge, pltpu.roll(cur, 1, 0))
    nxt = jnp.where(row == tm - 1, next_edge, pltpu.roll(cur, tm - 1, 0))
    zs = cur + (0.5 * (prev + nxt) - cur) * mu_ref[...]

    r = zs[:, :c]
    k = zs[:, c:2 * c]
    v = zs[:, 2 * c:3 * c]
    o = 3 * c
    wd = zs[:, o:o + 2 * LORA_PAD]
    ad = zs[:, o + 2 * LORA_PAD:o + 4 * LORA_PAD]
    gd = zs[:, o + 4 * LORA_PAD:]
    z_w = w0_ref[...] + _dot(jnp.tanh(wd).astype(BF16), w2_ref[...])
    lw = -DECAY_SCALE * jax.nn.sigmoid(z_w)
    a = jax.nn.sigmoid(a0_ref[...] + _dot(ad.astype(BF16), a2_ref[...]))
    g = _dot(jax.nn.sigmoid(gd).astype(BF16), g2_ref[...])
    kk = k * kkp_ref[...]
    ssq = _dot_exact_rhs(kk * kk, bd_ref[...])
    kk = kk * lax.rsqrt(ssq + 1e-12)
    r_ref[...] = r
    v_ref[...] = v
    kk_ref[...] = kk
    g_ref[...] = g
    for dr in range(2):
        a_d = a[:, dr * c:(dr + 1) * c]
        k_ref[dr] = k * (1.0 + (a_d - 1.0) * ka_ref[...])
        b_ref[dr] = kk * a_d
        lw_ref[dr] = lw[:, dr * c:(dr + 1) * c]


def _rwkv_prep_call(z, t_len, pw, tm_want=256):
    n, w = z.shape
    c = BRANCH_W
    tm = _row_tile(t_len, tm_want)
    tps = t_len // tm
    nblk8 = n // 8
    full = lambda shape: pl.BlockSpec(shape, lambda i: (0,) * len(shape))
    tile = pl.BlockSpec((tm, c), lambda i: (i, 0))
    tile2 = pl.BlockSpec((2, tm, c), lambda i: (0, i, 0))
    one = jax.ShapeDtypeStruct((n, c), F32)
    two = jax.ShapeDtypeStruct((2, n, c), F32)
    return pl.pallas_call(
        functools.partial(_rwkv_prep_kernel, tiles_per_seq=tps),
        out_shape=(one, one, one, one, two, two, two),
        grid=(n // tm,),
        in_specs=[pl.BlockSpec((tm, w), lambda i: (i, 0)),
                  pl.BlockSpec((8, w), lambda i: (jnp.maximum(i * (tm // 8) - 1, 0), 0)),
                  pl.BlockSpec((8, w), lambda i: (jnp.minimum((i + 1) * (tm // 8), nblk8 - 1), 0)),
                  full((1, w)), full((1, 2 * c)), full((2 * LORA_PAD, 2 * c)),
                  full((1, 2 * c)), full((2 * LORA_PAD, 2 * c)), full((GATE_LORA, c)),
                  full((1, c)), full((1, c)), full((c, c))],
        out_specs=(tile, tile, tile, tile, tile2, tile2, tile2),
        compiler_params=_params("parallel"),
        name="rwkv_prep",
    )(z, z, z, pw["mu"], pw["w0"], pw["w2"], pw["a0"], pw["a2"], pw["g2"],
      pw["kk"], pw["ka"], pw["bd"])


def _scan_kernel(r_ref, v_ref, kk_ref, k_ref, b_ref, lw_ref, s0_ref, o_ref, sfin_ref, s_ref,
                 *, nbatch):
    p = pl.program_id(0)
    n = pl.program_id(1)
    rev = (p // nbatch) == 1

    @pl.when(n == 0)
    def _():
        s_ref[...] = s0_ref[...]

    lw = lw_ref[...]
    cl = lw.shape[0]
    ti = lax.broadcasted_iota(jnp.int32, (cl, cl), 0)
    si = lax.broadcasted_iota(jnp.int32, (cl, cl), 1)
    tt = jnp.where(rev, cl - 1 - ti, ti)
    ss = jnp.where(rev, cl - 1 - si, si)
    strict = ss < tt
    incl = ss <= tt
    eye = ss == tt

    cum = _dot_exact_lhs(jnp.where(incl, 1.0, 0.0).astype(BF16), lw)
    tot = jnp.sum(lw, axis=0, keepdims=True)
    g_in = jnp.exp(cum)
    g_ex = jnp.exp(cum - lw)
    g_inv = jnp.exp(-cum)
    g_end = jnp.exp(tot - cum)
    g_tot = jnp.exp(tot)

    r = r_ref[...]
    v = v_ref[...]
    kk = kk_ref[...]
    k = k_ref[...]
    b = b_ref[...]
    qb = (r * g_in).astype(BF16)
    ab = (kk * g_ex).astype(BF16)
    kb = (k * g_inv).astype(BF16)
    bb = (b * g_inv).astype(BF16)
    kh = (k * g_end).astype(BF16)
    bh = (b * g_end).astype(BF16)
    vb = v.astype(BF16)

    nlev = cl.bit_length() - 1
    lev_masks = []
    for lev in range(nlev):
        lev_masks.append(((tt >> (lev + 1)) == (ss >> (lev + 1)))
                         & (((tt >> lev) & 1) == 1) & (((ss >> lev) & 1) == 0))

    outs = []
    for h in range(RWKV_HEADS):
        sl = slice(h * RWKV_HEAD, (h + 1) * RWKV_HEAD)
        a_h, q_h, k_h, b_h, v_h = ab[:, sl], qb[:, sl], kb[:, sl], bb[:, sl], vb[:, sl]
        s_h = s_ref[h]
        s_hb = s_h.astype(BF16)
        n_ab = jnp.where(strict, _dot(a_h, b_h, _NT), 0.0)
        a_ak = jnp.where(strict, _dot(a_h, k_h, _NT), 0.0).astype(BF16)
        a_qb = jnp.where(incl, _dot(q_h, b_h, _NT), 0.0).astype(BF16)
        a_qk = jnp.where(incl, _dot(q_h, k_h, _NT), 0.0).astype(BF16)
        t_inv = jnp.where(eye, 1.0, 0.0) - jnp.where(lev_masks[0], n_ab, 0.0)
        for lev in range(1, nlev):
            m_lev = jnp.where(lev_masks[lev], n_ab, 0.0)
            t_inv = t_inv - _dot3(t_inv, _dot3(m_lev, t_inv))
        rhs = _dot(a_h, s_hb, _NT) + _dot(a_ak, v_h)
        u = -_dot3(t_inv, rhs)
        ub = u.astype(BF16)
        outs.append(_dot(q_h, s_hb, _NT) + _dot(a_qb, ub) + _dot(a_qk, v_h))
        s_ref[h] = (s_h * g_tot[:, sl] + _dot(ub, bh[:, sl], _TN) + _dot(v_h, kh[:, sl], _TN))

    o_ref[...] = jnp.concatenate(outs, axis=1)

    @pl.when(n == pl.num_programs(1) - 1)
    def _():
        sfin_ref[...] = s_ref[...]


def _scan_call(prep, s0, nbatch, t_len):
    r, v, kk, _, k2, b2, lw2 = prep
    n, c = r.shape
    cl = min(SCAN_CHUNK, t_len)
    nc = t_len // cl

    def blk(p, i):
        fwd = (p // nbatch) == 0
        return (p % nbatch) * nc + jnp.where(fwd, i, nc - 1 - i)

    shared = pl.BlockSpec((cl, c), lambda p, i: (blk(p, i), 0))
    split = pl.BlockSpec((None, cl, c), lambda p, i: (p // nbatch, blk(p, i), 0))
    state = pl.BlockSpec((None, RWKV_HEADS, RWKV_HEAD, RWKV_HEAD), lambda p, i: (p, 0, 0, 0))
    return pl.pallas_call(
        functools.partial(_scan_kernel, nbatch=nbatch),
        out_shape=(jax.ShapeDtypeStruct((2, n, c), F32),
                   jax.ShapeDtypeStruct(s0.shape, F32)),
        grid=(2 * nbatch, nc),
        in_specs=[shared, shared, shared, split, split, split, state],
        out_specs=(split, state),
        scratch_shapes=[pltpu.VMEM((RWKV_HEADS, RWKV_HEAD, RWKV_HEAD), F32)],
        compiler_params=_params("parallel", "arbitrary"),
        name="rwkv_scan",
    )(r, v, kk, k2, b2, lw2, s0)


def _rwkv_out_kernel(o_ref, r_ref, k_ref, v_ref, g_ref, rk_ref, gg_ref, gb_ref, bd_ref, y_ref):
    bd = bd_ref[...]
    inv_n = 1.0 / RWKV_HEAD
    s = o_ref[0] + o_ref[1]
    mu = _dot_exact_rhs(s, bd) * inv_n
    sc = s - mu
    var = _dot_exact_rhs(sc * sc, bd) * inv_n
    y = sc * lax.rsqrt(var + GN_EPS) * gg_ref[...] + gb_ref[...]
    r = r_ref[...]
    bonus = _dot_exact_rhs(r * k_ref[0] * rk_ref[...] + r * k_ref[1] * rk_ref[...], bd)
    y_ref[...] = ((y + bonus * v_ref[...]) * g_ref[...]).astype(y_ref.dtype)


def _rwkv_out_call(o2, prep, pw, tm_want=512):
    r, v, _, g, k2, _, _ = prep
    n, c = r.shape
    tm = _row_tile(n, tm_want)
    tile = pl.BlockSpec((tm, c), lambda i: (i, 0))
    tile2 = pl.BlockSpec((2, tm, c), lambda i: (0, i, 0))
    vec = pl.BlockSpec((1, c), lambda i: (0, 0))
    return pl.pallas_call(
        _rwkv_out_kernel,
        out_shape=jax.ShapeDtypeStruct((n, c), BF16),
        grid=(n // tm,),
        in_specs=[tile2, tile, tile2, tile, tile, vec, vec, vec,
                  pl.BlockSpec((c, c), lambda i: (0, 0))],
        out_specs=tile,
        compiler_params=_params("parallel"),
        name="rwkv_out",
    )(o2, r, k2, v, g, pw["rk"], pw["gn_g"], pw["gn_b"], pw["bd"])


def _attn_prep_kernel(zw_ref, zd_ref, cos_ref, sin_ref, qn_ref, kn_ref,
                      qw_ref, kw_ref, vw_ref, qd_ref, kd_ref, vd_ref, *, rope):
    scale = HEAD_DIM ** -0.5
    hd = HEAD_DIM
    if rope:
        cosf = cos_ref[...]
        sins = sin_ref[...]
        lane = lax.broadcasted_iota(jnp.int32, cosf.shape, 1)
        first = (lane % (hd // 2)) < (hd // 4)

    def rot(xh):
        if not rope:
            return xh
        partner = jnp.where(first, pltpu.roll(xh, hd - hd // 4, 1), pltpu.roll(xh, hd // 4, 1))
        return xh * cosf + partner * sins

    def rms(xh, gain):
        return xh * lax.rsqrt(jnp.mean(xh * xh, axis=-1, keepdims=True) + RMS_EPS) * gain

    for h in range(Q_HEADS):
        sl = slice(h * hd, (h + 1) * hd)
        qw_ref[:, sl] = (rot(zw_ref[:, sl]) * scale).astype(BF16)
        qd_ref[:, sl] = (rot(rms(zd_ref[:, sl], qn_ref[...])) * scale).astype(BF16)
    for h in range(KV_HEADS):
        sl = slice(h * hd, (h + 1) * hd)
        ks = slice((Q_HEADS + h) * hd, (Q_HEADS + h + 1) * hd)
        vs = slice((Q_HEADS + KV_HEADS + h) * hd, (Q_HEADS + KV_HEADS + h + 1) * hd)
        kw_ref[:, sl] = rot(zw_ref[:, ks]).astype(BF16)
        kd_ref[:, sl] = rot(rms(zd_ref[:, ks], kn_ref[...])).astype(BF16)
        vw_ref[:, sl] = zw_ref[:, vs].astype(BF16)
        vd_ref[:, sl] = zd_ref[:, vs].astype(BF16)


def _attn_prep_call(zw, zd, t_len, cosf, sins, qn, kn, rope, tm_want=256):
    n = zw.shape[0]
    tm = _row_tile(t_len, tm_want)
    tps = t_len // tm
    hd = HEAD_DIM
    tile = lambda w: pl.BlockSpec((tm, w), lambda i: (i, 0))
    pos = pl.BlockSpec((tm, hd), lambda i: (i % tps, 0))
    vec = pl.BlockSpec((1, hd), lambda i: (0, 0))
    qs = jax.ShapeDtypeStruct((n, Q_HEADS * hd), BF16)
    ks = jax.ShapeDtypeStruct((n, KV_HEADS * hd), BF16)
    return pl.pallas_call(
        functools.partial(_attn_prep_kernel, rope=rope),
        out_shape=(qs, ks, ks, qs, ks, ks),
        grid=(n // tm,),
        in_specs=[tile(ATT_W), tile(ATT_W), pos, pos, vec, vec],
        out_specs=(tile(Q_HEADS * hd), tile(KV_HEADS * hd), tile(KV_HEADS * hd),
                   tile(Q_HEADS * hd), tile(KV_HEADS * hd), tile(KV_HEADS * hd)),
        compiler_params=_params("parallel"),
        name="attn_prep",
    )(zw, zd, cosf, sins, qn.reshape(1, hd), kn.reshape(1, hd))


def _full_attn_kernel(q_ref, k_ref, v_ref, *rest, sink):
    if sink:
        sink_ref, o_ref = rest
    else:
        (o_ref,) = rest
    hd = HEAD_DIM
    k = k_ref[...]
    v = v_ref[...]
    group = Q_HEADS // KV_HEADS
    for g in range(group):
        sl = slice(g * hd, (g + 1) * hd)
        s = _dot(q_ref[:, sl], k, _NT)
        m = jnp.max(s, axis=-1, keepdims=True)
        if sink:
            sk = sink_ref[:, g * hd:g * hd + 1]
            m = jnp.maximum(m, sk)
        p = jnp.exp(s - m)
        den = jnp.sum(p, axis=-1, keepdims=True)
        if sink:
            den = den + jnp.exp(sk - m)
        o_ref[:, sl] = (_dot(p.astype(BF16), v) / den).astype(o_ref.dtype)


def _full_attn_call(q, k, v, nbatch, sink_row=None, tq_want=256):
    n = q.shape[0]
    hd = HEAD_DIM
    tq_len = n // nbatch
    tk_len = k.shape[0] // nbatch
    tq = _row_tile(tq_len, tq_want)
    nq = tq_len // tq
    gw = (Q_HEADS // KV_HEADS) * hd
    in_specs = [pl.BlockSpec((tq, gw), lambda b, h, i: (b * nq + i, h)),
                pl.BlockSpec((tk_len, hd), lambda b, h, i: (b, h)),
                pl.BlockSpec((tk_len, hd), lambda b, h, i: (b, h))]
    args = [q, k, v]
    if sink_row is not None:
        in_specs.append(pl.BlockSpec((1, gw), lambda b, h, i: (0, h)))
        args.append(sink_row)
    return pl.pallas_call(
        functools.partial(_full_attn_kernel, sink=sink_row is not None),
        out_shape=jax.ShapeDtypeStruct((n, Q_HEADS * hd), BF16),
        grid=(nbatch, KV_HEADS, nq),
        in_specs=in_specs,
        out_specs=pl.BlockSpec((tq, gw), lambda b, h, i: (b * nq + i, h)),
        compiler_params=_params("parallel", "parallel", "arbitrary"),
        name="full_attn",
    )(*args)


def _win_attn_kernel(q_ref, kx_ref, vx_ref, kc_ref, vc_ref, sink_ref, o_ref):
    hd = HEAD_DIM
    nblk = pl.program_id(2)
    t_len = kx_ref.shape[0]
    span = 3 * BLOCK
    start = pl.multiple_of(jnp.clip((nblk - 1) * BLOCK, 0, t_len - span), BLOCK)
    kw = kx_ref[pl.ds(start, span), :]
    vw = vx_ref[pl.ds(start, span), :]
    kc = kc_ref[...]
    vc = vc_ref[...]
    qpos = nblk * BLOCK + lax.broadcasted_iota(jnp.int32, (BLOCK, span), 0)
    kpos = start + lax.broadcasted_iota(jnp.int32, (BLOCK, span), 1)
    allowed = jnp.abs(qpos - kpos) <= WIN
    group = Q_HEADS // KV_HEADS
    for g in range(group):
        sl = slice(g * hd, (g + 1) * hd)
        q = q_ref[:, sl]
        s_loc = jnp.where(allowed, _dot(q, kw, _NT), NEG_INF)
        s_ctx = _dot(q, kc, _NT)
        sk = sink_ref[:, g * hd:g * hd + 1]
        m = jnp.maximum(jnp.maximum(jnp.max(s_loc, axis=-1, keepdims=True),
                                    jnp.max(s_ctx, axis=-1, keepdims=True)), sk)
        p_loc = jnp.exp(s_loc - m)
        p_ctx = jnp.exp(s_ctx - m)
        den = (jnp.sum(p_loc, axis=-1, keepdims=True) + jnp.sum(p_ctx, axis=-1, keepdims=True)
               + jnp.exp(sk - m))
        o = _dot(p_ctx.astype(BF16), vc) + _dot(p_loc.astype(BF16), vw)
        o_ref[:, sl] = (o / den).astype(o_ref.dtype)


def _win_attn_call(q, kx, vx, kc, vc, sink_row, nbatch):
    n = q.shape[0]
    hd = HEAD_DIM
    t_len = n // nbatch
    l_len = kc.shape[0] // nbatch
    nb = t_len // BLOCK
    gw = (Q_HEADS // KV_HEADS) * hd
    return pl.pallas_call(
        _win_attn_kernel,
        out_shape=jax.ShapeDtypeStruct((n, Q_HEADS * hd), BF16),
        grid=(nbatch, KV_HEADS, nb),
        in_specs=[pl.BlockSpec((BLOCK, gw), lambda b, h, i: (b * nb + i, h)),
                  pl.BlockSpec((t_len, hd), lambda b, h, i: (b, h)),
                  pl.BlockSpec((t_len, hd), lambda b, h, i: (b, h)),
                  pl.BlockSpec((l_len, hd), lambda b, h, i: (b, h)),
                  pl.BlockSpec((l_len, hd), lambda b, h, i: (b, h)),
                  pl.BlockSpec((1, gw), lambda b, h, i: (0, h))],
        out_specs=pl.BlockSpec((BLOCK, gw), lambda b, h, i: (b * nb + i, h)),
        compiler_params=_params("parallel", "parallel", "arbitrary"),
        name="win_attn",
    )(q, kx, vx, kc, vc, sink_row)


def _merge_kernel(x_ref, gt_ref, b0_ref, b1_ref, b2_ref, b3_ref, gates_ref, wup_ref, wout_ref,
                  lg_ref, lb_ref, o_ref, acc_ref):
    kb = pl.program_id(1)
    for i, br in enumerate((b0_ref, b1_ref, b2_ref, b3_ref)):
        @pl.when(kb == i)
        def _(i=i, br=br):
            y = gates_ref[...] * _dot(br[...], wup_ref[0])
            if i == 0:
                acc_ref[...] = y
            else:
                acc_ref[...] += y

    @pl.when(kb == N_BRANCH - 1)
    def _():
        out = _dot(acc_ref[...].astype(BF16), wout_ref[...])
        y = DEEPNORM_ALPHA * x_ref[...] + gt_ref[0] * out
        o_ref[...] = _layer_norm(y, lg_ref[...], lb_ref[...])


def _merge_call(x, mod_rows, row_of_tile_fn, branches, gates, wup, wout, lg, lb, tm_want=512):
    n, d = x.shape
    c = BRANCH_W
    tm = _row_tile(n, tm_want)
    rot = functools.partial(row_of_tile_fn, tm)
    br_spec = pl.BlockSpec((tm, c), lambda i, j: (i, 0))
    return pl.pallas_call(
        _merge_kernel,
        out_shape=jax.ShapeDtypeStruct((n, d), F32),
        grid=(n // tm, N_BRANCH),
        in_specs=[pl.BlockSpec((tm, d), lambda i, j: (i, 0)),
                  _mod_spec(d, rot, 5),
                  br_spec, br_spec, br_spec, br_spec,
                  pl.BlockSpec((tm, d), lambda i, j: (i, j)),
                  pl.BlockSpec((1, c, d), lambda i, j: (j, 0, 0)),
                  pl.BlockSpec((d, d), lambda i, j: (0, 0)),
                  pl.BlockSpec((1, d), lambda i, j: (0, 0)),
                  pl.BlockSpec((1, d), lambda i, j: (0, 0))],
        out_specs=pl.BlockSpec((tm, d), lambda i, j: (i, 0)),
        scratch_shapes=[pltpu.VMEM((tm, d), F32)],
        compiler_params=_params("parallel", "arbitrary"),
        name="merge",
    )(x, mod_rows, *branches, gates, wup, wout, lg.reshape(1, d), lb.reshape(1, d))


def _pad_rows(w, rows):
    return jnp.pad(w, ((0, rows - w.shape[0]), (0, 0)))


def _rwkv_weights(w_rwkv, mu, w0, w2, a0, a2, g2, kk, ka, rk, gn_g, gn_b):
    c = BRANCH_W
    o = 3 * c
    cuts = [o, o + DECAY_LORA, o + 2 * DECAY_LORA, o + 2 * DECAY_LORA + ICLR_LORA,
            o + 2 * DECAY_LORA + 2 * ICLR_LORA]

    def pad_cols(m):
        parts = [m[:, :o]]
        for lo, hi in zip(cuts[:-1], cuts[1:]):
            parts.append(jnp.pad(m[:, lo:hi], ((0, 0), (0, LORA_PAD - (hi - lo)))))
        parts.append(m[:, cuts[-1]:])
        return jnp.concatenate(parts, axis=1)

    def block_diag(m):
        z = jnp.zeros((LORA_PAD, c), m.dtype)
        top = jnp.concatenate([_pad_rows(m[0], LORA_PAD), z], axis=1)
        bot = jnp.concatenate([z, _pad_rows(m[1], LORA_PAD)], axis=1)
        return jnp.concatenate([top, bot], axis=0)

    head = jnp.arange(c) // RWKV_HEAD
    return {
        "w": pad_cols(w_rwkv).astype(BF16),
        "mu": pad_cols(mu.reshape(1, -1)),
        "w0": w0.reshape(1, 2 * c), "w2": block_diag(w2).astype(BF16),
        "a0": a0.reshape(1, 2 * c), "a2": block_diag(a2).astype(BF16),
        "g2": g2.astype(BF16),
        "kk": kk.reshape(1, c), "ka": ka.reshape(1, c), "rk": rk.reshape(1, c),
        "gn_g": gn_g.reshape(1, c), "gn_b": gn_b.reshape(1, c),
        "bd": (head[:, None] == head[None, :]).astype(BF16),
    }


def _rope_tables(t_len):
    rows = t_len // GRID_W
    row = jnp.repeat(jnp.arange(rows), GRID_W).astype(F32)
    col = (jnp.arange(t_len) % GRID_W).astype(F32)
    n_freq = HEAD_DIM // 4
    inv = ROPE_THETA ** (-jnp.arange(n_freq, dtype=F32) / n_freq)
    ar = row[:, None] * inv
    ac = col[:, None] * inv
    cosf = jnp.concatenate([jnp.cos(ar), jnp.cos(ar), jnp.cos(ac), jnp.cos(ac)], axis=1)
    sins = jnp.concatenate([-jnp.sin(ar), jnp.sin(ar), -jnp.sin(ac), jnp.sin(ac)], axis=1)
    return cosf, sins


def kernel(x, c, ctx, c_ctx, w_mod, b_mod, ln_g, ln_b, ffn1_wi, ffn1_wo, ffn2_wi, ffn2_wo, w_in, b_gate, pool_w, pool_scale, rwkv_mu, rwkv_w0, rwkv_w2, rwkv_a0, rwkv_a2, rwkv_g2, rwkv_kk, rwkv_ka, rwkv_rk, rwkv_gn_g, rwkv_gn_b, c_sink, d_qnorm, d_knorm, w_up, w_out):
    nb, t_len, d = x.shape
    l_len = ctx.shape[1]
    depth = w_mod.shape[0]
    cw = BRANCH_W
    assert depth == DEPTH and t_len % GRID_W == 0 and t_len >= 3 * BLOCK

    mod_pad = 16
    c_all = jnp.zeros((mod_pad, d), F32).at[:nb].set(c).at[nb].set(c_ctx)
    cosf, sins = _rope_tables(t_len)
    s0 = jnp.zeros((2 * nb, RWKV_HEADS, RWKV_HEAD, RWKV_HEAD), F32)

    def row_x(tm, i):
        return (i * tm) // t_len

    def row_c(tm, i):
        return nb

    xs = x.reshape(nb * t_len, d)
    xc = ctx.reshape(nb * l_len, d)
    o1 = cw
    o2 = o1 + 3 * cw + 2 * DECAY_LORA + 2 * ICLR_LORA + GATE_LORA
    o3 = o2 + ATT_W
    o4 = o3 + ATT_W

    for l in range(depth):
        last = l == depth - 1
        mod_rows = _mod_call(c_all, w_mod[l], b_mod[l]).reshape(mod_pad * N_MOD, 1, d)
        wi1, wo1 = ffn1_wi[l].astype(BF16), ffn1_wo[l].astype(BF16)
        wi2, wo2 = ffn2_wi[l].astype(BF16), ffn2_wo[l].astype(BF16)
        w_pool = w_in[l][:, :o1].astype(BF16)
        pw = _rwkv_weights(w_in[l][:, o1:o2], rwkv_mu[l], rwkv_w0[l], rwkv_w2[l], rwkv_a0[l],
                           rwkv_a2[l], rwkv_g2[l], rwkv_kk[l], rwkv_ka[l], rwkv_rk[l],
                           rwkv_gn_g[l], rwkv_gn_b[l])
        w_win = w_in[l][:, o2:o3].astype(BF16)
        w_dense = w_in[l][:, o3:o4].astype(BF16)
        w_gate = w_in[l][:, o4:].astype(BF16)
        pool_wb = pool_w[l].astype(BF16)
        wup = w_up[l].astype(BF16)
        wout = w_out[l].astype(BF16)
        sink_row = jnp.repeat(c_sink[l], HEAD_DIM).reshape(1, Q_HEADS * HEAD_DIM)

        xs = _ffn_call(xs, mod_rows, row_x, 0, wi1, wo1, ln_g[l, 0], ln_b[l, 0])
        xc = _ffn_call(xc, mod_rows, row_c, 0, wi1, wo1, ln_g[l, 0], ln_b[l, 0])

        proj_x = lambda w, bias=None: _proj_call(xs, mod_rows, row_x, 3, w, bias)
        proj_c = lambda w, bias=None: _proj_call(xc, mod_rows, row_c, 3, w, bias)

        prep_c = _rwkv_prep_call(proj_c(pw["w"]), l_len, pw)
        prep_x = _rwkv_prep_call(proj_x(pw["w"]), t_len, pw)
        o_c, s_ctx = _scan_call(prep_c, s0, nb, l_len)
        o_x, _ = _scan_call(prep_x, s_ctx, nb, t_len)
        yb_x = _rwkv_out_call(o_x, prep_x, pw)

        qw_c, kw_c, vw_c, qd_c, kd_c, vd_c = _attn_prep_call(
            proj_c(w_win), proj_c(w_dense), l_len, cosf, sins, d_qnorm[l], d_knorm[l], rope=False)
        qw_x, kw_x, vw_x, qd_x, kd_x, vd_x = _attn_prep_call(
            proj_x(w_win), proj_x(w_dense), t_len, cosf, sins, d_qnorm[l], d_knorm[l], rope=True)
        yc_x = _win_attn_call(qw_x, kw_x, vw_x, kw_c, vw_c, sink_row, nb)
        kv_w = KV_HEADS * HEAD_DIM
        kd_all = jnp.concatenate([kd_c.reshape(nb, l_len, kv_w), kd_x.reshape(nb, t_len, kv_w)],
                                 axis=1).reshape(nb * (l_len + t_len), kv_w)
        vd_all = jnp.concatenate([vd_c.reshape(nb, l_len, kv_w), vd_x.reshape(nb, t_len, kv_w)],
                                 axis=1).reshape(nb * (l_len + t_len), kv_w)
        yd_x = _full_attn_call(qd_x, kd_all, vd_all, nb)

        ya_x = _pool_call(proj_x(w_pool), nb, pool_wb, pool_scale[l])

        gates_x = proj_x(w_gate, b_gate[l])
        xs = _merge_call(xs, mod_rows, row_x, (ya_x, yb_x, yc_x, yd_x), gates_x, wup, wout,
                         ln_g[l, 1], ln_b[l, 1])
        xs = _ffn_call(xs, mod_rows, row_x, 6, wi2, wo2, ln_g[l, 2], ln_b[l, 2])

        if not last:
            ya_c = _pool_call(proj_c(w_pool), nb, pool_wb, pool_scale[l])
            yb_c = _rwkv_out_call(o_c, prep_c, pw)
            yc_c = _full_attn_call(qw_c, kw_c, vw_c, nb, sink_row)
            yd_c = _full_attn_call(qd_c, kd_c, vd_c, nb)
            gates_c = proj_c(w_gate, b_gate[l])
            xc = _merge_call(xc, mod_rows, row_c, (ya_c, yb_c, yc_c, yd_c), gates_c, wup, wout,
                             ln_g[l, 1], ln_b[l, 1])
            xc = _ffn_call(xc, mod_rows, row_c, 6, wi2, wo2, ln_g[l, 2], ln_b[l, 2])

    return xs.reshape(nb, t_len, d)
```

```python
import functools
import math

import jax
import jax.numpy as jnp
from jax import lax
from jax.experimental import pallas as pl
from jax.experimental.pallas import tpu as pltpu

F32 = jnp.float32
BF16 = jnp.bfloat16

GRID_W = 64
HEAD_DIM = 128
N_BRANCH = 4
BRANCH_W = 512
POOL_WINDOWS = (2, 4, 8, 16)
POOL_GROUP = BRANCH_W // len(POOL_WINDOWS)
RWKV_HEAD = 64
RWKV_HEADS = BRANCH_W // RWKV_HEAD
DECAY_LORA = 96
ICLR_LORA = 96
GATE_LORA = 256
LORA_PAD = 128
RWKV_W = 3 * BRANCH_W + 4 * LORA_PAD + GATE_LORA
WIN = 128
BLOCK = 128
Q_HEADS = 4
KV_HEADS = 2
ATT_W = (Q_HEADS + 2 * KV_HEADS) * HEAD_DIM
N_MOD = 9
DEPTH = 2
ROPE_THETA = 10000.0
DEEPNORM_ALPHA = (2 * DEPTH) ** 0.25
LN_EPS = 1e-6
RMS_EPS = 1e-6
GN_EPS = 64e-5
NEG_INF = -1e30
DECAY_SCALE = math.exp(-0.5)

V7X_VMEM_BYTES = 64 * 1024 * 1024
VMEM_LIMIT = V7X_VMEM_BYTES - 8 * 1024 * 1024
SCAN_CHUNK = 64

_NT = (((1,), (1,)), ((), ()))
_TN = (((0,), (0,)), ((), ()))


def _params(*sem):
    return pltpu.CompilerParams(dimension_semantics=sem, vmem_limit_bytes=VMEM_LIMIT)


def _dot(a, b, dims=None):
    if dims is None:
        return jnp.dot(a, b, preferred_element_type=F32)
    return lax.dot_general(a, b, dims, preferred_element_type=F32)


def _split2(a):
    hi = a.astype(BF16)
    lo = (a - hi.astype(F32)).astype(BF16)
    return hi, lo


def _dot3(a, b, dims=None):
    ah, al = _split2(a)
    bh, bl = _split2(b)
    return _dot(ah, bh, dims) + (_dot(ah, bl, dims) + _dot(al, bh, dims))


def _dot_exact_lhs(a_bf16, b):
    b0 = b.astype(BF16)
    r1 = b - b0.astype(F32)
    b1 = r1.astype(BF16)
    b2 = (r1 - b1.astype(F32)).astype(BF16)
    return _dot(a_bf16, b0) + (_dot(a_bf16, b1) + _dot(a_bf16, b2))


def _dot_exact_rhs(a, b_bf16):
    a0 = a.astype(BF16)
    r1 = a - a0.astype(F32)
    a1 = r1.astype(BF16)
    a2 = (r1 - a1.astype(F32)).astype(BF16)
    return _dot(a0, b_bf16) + (_dot(a1, b_bf16) + _dot(a2, b_bf16))


def _layer_norm(y, g, b):
    mu = jnp.mean(y, axis=-1, keepdims=True)
    yc = y - mu
    var = jnp.mean(yc * yc, axis=-1, keepdims=True)
    return yc * lax.rsqrt(var + LN_EPS) * g + b


def _row_tile(n, want):
    t = min(want, n)
    assert n % t == 0, (n, t)
    return t


def _mod_kernel(c_ref, w_ref, b_ref, o_ref):
    c = c_ref[...]
    s = c * jax.nn.sigmoid(c)
    o_ref[...] = _dot(s.astype(BF16), w_ref[...].astype(BF16)) + b_ref[...]


def _mod_call(c_all, w, b):
    rows, d = c_all.shape
    nout = w.shape[1]
    tn = _row_tile(nout, min(d, 1024))
    return pl.pallas_call(
        _mod_kernel,
        out_shape=jax.ShapeDtypeStruct((rows, nout), F32),
        grid=(nout // tn,),
        in_specs=[pl.BlockSpec((rows, d), lambda j: (0, 0)),
                  pl.BlockSpec((d, tn), lambda j: (0, j)),
                  pl.BlockSpec((1, tn), lambda j: (0, j))],
        out_specs=pl.BlockSpec((rows, tn), lambda j: (0, j)),
        compiler_params=_params("arbitrary"),
        name="mod",
    )(c_all, w, b.reshape(1, nout))


def _mod_spec(d, row_of_tile, k):
    return pl.BlockSpec((1, 1, d), lambda i, j: (row_of_tile(i) * N_MOD + k, 0, 0))


def _ffn_kernel(x_ref, sh_ref, sc_ref, gt_ref, wg_ref, wu_ref, wo_ref, lg_ref, lb_ref,
                o_ref, xm_ref, acc_ref):
    j = pl.program_id(1)

    @pl.when(j == 0)
    def _():
        xm_ref[...] = (x_ref[...] * (1.0 + sc_ref[0]) + sh_ref[0]).astype(BF16)
        acc_ref[...] = jnp.zeros_like(acc_ref)

    xm = xm_ref[...]
    hg = _dot(xm, wg_ref[...])
    hu = _dot(xm, wu_ref[...])
    a = (hg * jax.nn.sigmoid(hg) * hu).astype(BF16)
    acc_ref[...] += _dot(a, wo_ref[...])

    @pl.when(j == pl.num_programs(1) - 1)
    def _():
        y = DEEPNORM_ALPHA * x_ref[...] + gt_ref[0] * (0.5 * acc_ref[...])
        o_ref[...] = _layer_norm(y, lg_ref[...], lb_ref[...])


def _ffn_call(x, mod_rows, row_of_tile_fn, k0, wi, wo, lg, lb, tm_want=512, tf_want=512):
    n, d = x.shape
    dff = wo.shape[0]
    tm = _row_tile(n, tm_want)
    tf = _row_tile(dff, tf_want)
    nj = dff // tf
    rot = functools.partial(row_of_tile_fn, tm)
    return pl.pallas_call(
        _ffn_kernel,
        out_shape=jax.ShapeDtypeStruct((n, d), F32),
        grid=(n // tm, nj),
        in_specs=[pl.BlockSpec((tm, d), lambda i, j: (i, 0)),
                  _mod_spec(d, rot, k0), _mod_spec(d, rot, k0 + 1), _mod_spec(d, rot, k0 + 2),
                  pl.BlockSpec((d, tf), lambda i, j: (0, j)),
                  pl.BlockSpec((d, tf), lambda i, j: (0, j + nj)),
                  pl.BlockSpec((tf, d), lambda i, j: (j, 0)),
                  pl.BlockSpec((1, d), lambda i, j: (0, 0)),
                  pl.BlockSpec((1, d), lambda i, j: (0, 0))],
        out_specs=pl.BlockSpec((tm, d), lambda i, j: (i, 0)),
        scratch_shapes=[pltpu.VMEM((tm, d), BF16), pltpu.VMEM((tm, d), F32)],
        compiler_params=_params("parallel", "arbitrary"),
        name="ffn",
    )(x, mod_rows, mod_rows, mod_rows, wi, wi, wo, lg.reshape(1, d), lb.reshape(1, d))


def _proj_kernel(x_ref, sh_ref, sc_ref, w_ref, *rest, gate):
    if gate:
        b_ref, o_ref, xm_ref = rest
    else:
        o_ref, xm_ref = rest

    @pl.when(pl.program_id(1) == 0)
    def _():
        xm_ref[...] = (x_ref[...] * (1.0 + sc_ref[0]) + sh_ref[0]).astype(BF16)

    z = _dot(xm_ref[...], w_ref[...])
    if gate:
        z = jax.nn.sigmoid(z + b_ref[...])
    o_ref[...] = z.astype(o_ref.dtype)


def _proj_call(x, mod_rows, row_of_tile_fn, k0, w, bias=None, tm_want=512, tn_want=1024):
    n, d = x.shape
    nout = w.shape[1]
    tm = _row_tile(n, tm_want)
    tn = nout if nout % tn_want else tn_want
    rot = functools.partial(row_of_tile_fn, tm)
    in_specs = [pl.BlockSpec((tm, d), lambda i, j: (i, 0)),
                _mod_spec(d, rot, k0), _mod_spec(d, rot, k0 + 1),
                pl.BlockSpec((d, tn), lambda i, j: (0, j))]
    args = [x, mod_rows, mod_rows, w]
    if bias is not None:
        in_specs.append(pl.BlockSpec((1, tn), lambda i, j: (0, j)))
        args.append(bias.reshape(1, nout))
    return pl.pallas_call(
        functools.partial(_proj_kernel, gate=bias is not None),
        out_shape=jax.ShapeDtypeStruct((n, nout), F32),
        grid=(n // tm, nout // tn),
        in_specs=in_specs,
        out_specs=pl.BlockSpec((tm, tn), lambda i, j: (i, j)),
        scratch_shapes=[pltpu.VMEM((tm, d), BF16)],
        compiler_params=_params("parallel", "arbitrary"),
        name="proj",
    )(*args)


def _pool_kernel(z_ref, w_ref, s_ref, o_ref):
    g = pl.program_id(1)
    u = z_ref[...]
    t_len = u.shape[0]
    t = lax.broadcasted_iota(jnp.int32, u.shape, 0)
    for gi, win in enumerate(POOL_WINDOWS):
        @pl.when(g == gi)
        def _(win=win):
            half = win // 2
            acc = u
            for dlt in range(-half, half):
                if dlt == 0:
                    continue
                shifted = pltpu.roll(u, (-dlt) % t_len, 0)
                ok = (t + dlt >= 0) & (t + dlt < t_len)
                acc = acc + jnp.where(ok, shifted, 0.0)
            cnt = (jnp.minimum(t + half, t_len) - jnp.maximum(t - half, 0)).astype(F32)
            pooled = acc / cnt - u
            y = _dot(pooled.astype(BF16), w_ref[0]) * s_ref[...]
            o_ref[...] = y.astype(o_ref.dtype)


def _pool_call(z, nseq, w_bf16, scale):
    n, c = z.shape
    t_len = n // nseq
    ng = len(POOL_WINDOWS)
    return pl.pallas_call(
        _pool_kernel,
        out_shape=jax.ShapeDtypeStruct((n, c), BF16),
        grid=(nseq, ng),
        in_specs=[pl.BlockSpec((t_len, POOL_GROUP), lambda b, g: (b, g)),
                  pl.BlockSpec((1, POOL_GROUP, POOL_GROUP), lambda b, g: (g, 0, 0)),
                  pl.BlockSpec((1, POOL_GROUP), lambda b, g: (0, g))],
        out_specs=pl.BlockSpec((t_len, POOL_GROUP), lambda b, g: (b, g)),
        compiler_params=_params("parallel", "arbitrary"),
        name="pool",
    )(z, w_bf16, scale.reshape(1, c))


def _rwkv_prep_kernel(z_ref, zp_ref, zn_ref, mu_ref, w0_ref, w2_ref, a0_ref, a2_ref, g2_ref,
                      kkp_ref, ka_ref, bd_ref,
                      r_ref, v_ref, kk_ref, g_ref, k_ref, b_ref, lw_ref, *, tiles_per_seq):
    i = pl.program_id(0)
    cur = z_ref[...]
    tm = cur.shape[0]
    c = BRANCH_W
    row = lax.broadcasted_iota(jnp.int32, cur.shape, 0)
    pos = i % tiles_per_seq
    prev_edge = jnp.where(pos == 0, 0.0, zp_ref[7:8, :])
    next_edge = jnp.where(pos == tiles_per_seq - 1, 0.0, zn_ref[0:1, :])
    prev = jnp.where(row == 0, prev_edge, pltpu.roll(cur, 1, 0))
    nxt = jnp.where(row == tm - 1, next_edge, pltpu.roll(cur, tm - 1, 0))
    zs = cur + (0.5 * (prev + nxt) - cur) * mu_ref[...]

    r = zs[:, :c]
    k = zs[:, c:2 * c]
    v = zs[:, 2 * c:3 * c]
    o = 3 * c
    wd = zs[:, o:o + 2 * LORA_PAD]
    ad = zs[:, o + 2 * LORA_PAD:o + 4 * LORA_PAD]
    gd = zs[:, o + 4 * LORA_PAD:]
    z_w = w0_ref[...] + _dot(jnp.tanh(wd).astype(BF16), w2_ref[...])
    lw = -DECAY_SCALE * jax.nn.sigmoid(z_w)
    a = jax.nn.sigmoid(a0_ref[...] + _dot(ad.astype(BF16), a2_ref[...]))
    g = _dot(jax.nn.sigmoid(gd).astype(BF16), g2_ref[...])
    kk = k * kkp_ref[...]
    ssq = _dot_exact_rhs(kk * kk, bd_ref[...])
    kk = kk * lax.rsqrt(ssq + 1e-12)
    r_ref[...] = r
    v_ref[...] = v
    kk_ref[...] = kk
    g_ref[...] = g
    for dr in range(2):
        a_d = a[:, dr * c:(dr + 1) * c]
        k_ref[dr] = k * (1.0 + (a_d - 1.0) * ka_ref[...])
        b_ref[dr] = kk * a_d
        lw_ref[dr] = lw[:, dr * c:(dr + 1) * c]


def _rwkv_prep_call(z, t_len, pw, tm_want=256):
    n, w = z.shape
    c = BRANCH_W
    tm = _row_tile(t_len, tm_want)
    tps = t_len // tm
    nblk8 = n // 8
    full = lambda shape: pl.BlockSpec(shape, lambda i: (0,) * len(shape))
    tile = pl.BlockSpec((tm, c), lambda i: (i, 0))
    tile2 = pl.BlockSpec((2, tm, c), lambda i: (0, i, 0))
    one = jax.ShapeDtypeStruct((n, c), F32)
    two = jax.ShapeDtypeStruct((2, n, c), F32)
    return pl.pallas_call(
        functools.partial(_rwkv_prep_kernel, tiles_per_seq=tps),
        out_shape=(one, one, one, one, two, two, two),
        grid=(n // tm,),
        in_specs=[pl.BlockSpec((tm, w), lambda i: (i, 0)),
                  pl.BlockSpec((8, w), lambda i: (jnp.maximum(i * (tm // 8) - 1, 0), 0)),
                  pl.BlockSpec((8, w), lambda i: (jnp.minimum((i + 1) * (tm // 8), nblk8 - 1), 0)),
                  full((1, w)), full((1, 2 * c)), full((2 * LORA_PAD, 2 * c)),
                  full((1, 2 * c)), full((2 * LORA_PAD, 2 * c)), full((GATE_LORA, c)),
                  full((1, c)), full((1, c)), full((c, c))],
        out_specs=(tile, tile, tile, tile, tile2, tile2, tile2),
        compiler_params=_params("parallel"),
        name="rwkv_prep",
    )(z, z, z, pw["mu"], pw["w0"], pw["w2"], pw["a0"], pw["a2"], pw["g2"],
      pw["kk"], pw["ka"], pw["bd"])


def _block_diag4(x_bf16, head_mask):
    return jnp.concatenate([x_bf16] * 4, axis=0) * head_mask


def _dot3_bd(a, x, head_mask):
    cl = a.shape[0]
    ah, al = _split2(a)
    xh, xl = _split2(x)
    p = _dot(jnp.concatenate([ah, al], axis=0), _block_diag4(xh, head_mask))
    return p[:cl] + (p[cl:] + _dot(ah, _block_diag4(xl, head_mask)))


def _scan_masks(rev, cl):
    hw = RWKV_HEAD
    gw = 4 * hw
    row = lax.broadcasted_iota(jnp.int32, (cl, gw), 0)
    lane = lax.broadcasted_iota(jnp.int32, (cl, gw), 1)
    pos_t = (cl - 1 - row) if rev else row
    pos_s = lane % hw
    pos_s = (cl - 1 - pos_s) if rev else pos_s
    nlev = cl.bit_length() - 1
    brow = lax.broadcasted_iota(jnp.int32, (gw, gw), 0) // hw
    blane = lax.broadcasted_iota(jnp.int32, (gw, gw), 1) // hw
    ti = lax.broadcasted_iota(jnp.int32, (cl, cl), 0)
    si = lax.broadcasted_iota(jnp.int32, (cl, cl), 1)
    return {
        "lane_head": lane // hw,
        "strict": pos_s < pos_t,
        "incl": pos_s <= pos_t,
        "eye": jnp.where(pos_s == pos_t, 1.0, 0.0),
        "levels": [((pos_t >> (lev + 1)) == (pos_s >> (lev + 1)))
                   & (((pos_t >> lev) & 1) == 1) & (((pos_s >> lev) & 1) == 0)
                   for lev in range(nlev)],
        "head": jnp.where(brow == blane, 1.0, 0.0).astype(BF16),
        "tri": jnp.where((si >= ti) if rev else (si <= ti), 1.0, 0.0).astype(BF16),
    }


def _scan_group(mk, ab, qb, kb, bb, kh, bh, vb, g_tot, s_ref, sl, out):
    cl = ab.shape[0]
    hw = RWKV_HEAD
    hm = mk["head"]
    aq = jnp.concatenate([ab, qb], axis=0)
    p_b = _dot(aq, _block_diag4(bb, hm), _NT)
    p_k = _dot(aq, _block_diag4(kb, hm), _NT)
    n_ab = jnp.where(mk["strict"], p_b[:cl], 0.0)
    a_ak = jnp.where(mk["strict"], p_k[:cl], 0.0).astype(BF16)
    a_qb = jnp.where(mk["incl"], p_b[cl:], 0.0).astype(BF16)
    a_qk = jnp.where(mk["incl"], p_k[cl:], 0.0).astype(BF16)
    yield
    t_inv = mk["eye"] - jnp.where(mk["levels"][0], n_ab, 0.0)
    for lev in range(1, len(mk["levels"])):
        x = _dot3_bd(jnp.where(mk["levels"][lev], n_ab, 0.0), t_inv, hm)
        yield
        t_inv = t_inv - _dot3_bd(t_inv, x, hm)
        yield
    s_g = s_ref[:, sl]
    p_s = _dot(aq, _block_diag4(s_g.astype(BF16), hm), _NT)
    p_v = _dot(jnp.concatenate([a_ak, a_qk], axis=0), _block_diag4(vb, hm))
    yield
    ub = (-_dot3_bd(t_inv, p_s[:cl] + p_v[:cl], hm)).astype(BF16)
    yield
    out.append(p_s[cl:] + p_v[cl:] + _dot(a_qb, _block_diag4(ub, hm)))
    full = _dot(jnp.concatenate([ub, vb], axis=0), jnp.concatenate([bh, kh], axis=0), _TN)
    upd = jnp.where(mk["lane_head"] == 0, full[:hw], 0.0)
    for h in range(1, 4):
        upd = upd + jnp.where(mk["lane_head"] == h, full[h * hw:(h + 1) * hw], 0.0)
    s_ref[:, sl] = s_g * g_tot + upd


def _scan_direction(rev, r, v, kk, k, b, lw, s_ref, out):
    cl, c = lw.shape
    gw = 4 * RWKV_HEAD
    assert cl == RWKV_HEAD
    mk = _scan_masks(rev, cl)
    cum = _dot_exact_lhs(mk["tri"], lw)
    tot = jnp.sum(lw, axis=0, keepdims=True)
    g_inv = jnp.exp(-cum)
    g_end = jnp.exp(tot - cum)
    g_tot = jnp.exp(tot)
    qb = (r * jnp.exp(cum)).astype(BF16)
    ab = (kk * jnp.exp(cum - lw)).astype(BF16)
    kb = (k * g_inv).astype(BF16)
    bb = (b * g_inv).astype(BF16)
    kh = (k * g_end).astype(BF16)
    bh = (b * g_end).astype(BF16)
    vb = v.astype(BF16)
    gens = []
    for g in range(c // gw):
        sl = slice(g * gw, (g + 1) * gw)
        gens.append(_scan_group(mk, ab[:, sl], qb[:, sl], kb[:, sl], bb[:, sl], kh[:, sl], bh[:, sl],
                                vb[:, sl], g_tot[:, sl], s_ref, sl, out))
    return gens


def _scan_kernel(rf_ref, rb_ref, vf_ref, vb_ref, kkf_ref, kkb_ref, kf_ref, kb_ref, bf_ref, bb_ref,
                 lwf_ref, lwb_ref, s0_ref, of_ref, ob_ref, sfin_ref, s_ref):
    n = pl.program_id(1)

    @pl.when(n == 0)
    def _():
        s_ref[...] = s0_ref[...]

    out_f, out_b = [], []
    gens = _scan_direction(False, rf_ref[...], vf_ref[...], kkf_ref[...], kf_ref[...],
                           bf_ref[...], lwf_ref[...], s_ref.at[0], out_f)
    gens += _scan_direction(True, rb_ref[...], vb_ref[...], kkb_ref[...], kb_ref[...],
                            bb_ref[...], lwb_ref[...], s_ref.at[1], out_b)
    while gens:
        alive = []
        for gen in gens:
            try:
                next(gen)
                alive.append(gen)
            except StopIteration:
                pass
        gens = alive
    of_ref[...] = jnp.concatenate(out_f, axis=1)
    ob_ref[...] = jnp.concatenate(out_b, axis=1)

    @pl.when(n == pl.num_programs(1) - 1)
    def _():
        sfin_ref[...] = s_ref[...]


def _scan_call(prep, s0, nbatch, t_len):
    r, v, kk, _, k2, b2, lw2 = prep
    n, c = r.shape
    cl = SCAN_CHUNK
    nc = t_len // cl
    fwd = pl.BlockSpec((cl, c), lambda bi, i: (bi * nc + i, 0))
    bwd = pl.BlockSpec((cl, c), lambda bi, i: (bi * nc + nc - 1 - i, 0))
    fwd2 = pl.BlockSpec((None, cl, c), lambda bi, i: (0, bi * nc + i, 0))
    bwd2 = pl.BlockSpec((None, cl, c), lambda bi, i: (1, bi * nc + nc - 1 - i, 0))
    state = pl.BlockSpec((2, None, RWKV_HEAD, c), lambda bi, i: (0, bi, 0, 0))
    return pl.pallas_call(
        _scan_kernel,
        out_shape=(jax.ShapeDtypeStruct((n, c), F32), jax.ShapeDtypeStruct((n, c), F32),
                   jax.ShapeDtypeStruct(s0.shape, F32)),
        grid=(nbatch, nc),
        in_specs=[fwd, bwd, fwd, bwd, fwd, bwd, fwd2, bwd2, fwd2, bwd2, fwd2, bwd2, state],
        out_specs=(fwd, bwd, state),
        scratch_shapes=[pltpu.VMEM((2, RWKV_HEAD, c), F32)],
        compiler_params=_params("parallel", "arbitrary"),
        name="rwkv_scan",
    )(r, r, v, v, kk, kk, k2, k2, b2, b2, lw2, lw2, s0)


def _rwkv_out_kernel(of_ref, ob_ref, r_ref, k_ref, v_ref, g_ref, rk_ref, gg_ref, gb_ref, bd_ref, y_ref):
    bd = bd_ref[...]
    inv_n = 1.0 / RWKV_HEAD
    s = of_ref[...] + ob_ref[...]
    mu = _dot_exact_rhs(s, bd) * inv_n
    sc = s - mu
    var = _dot_exact_rhs(sc * sc, bd) * inv_n
    y = sc * lax.rsqrt(var + GN_EPS) * gg_ref[...] + gb_ref[...]
    r = r_ref[...]
    bonus = _dot_exact_rhs(r * k_ref[0] * rk_ref[...] + r * k_ref[1] * rk_ref[...], bd)
    y_ref[...] = ((y + bonus * v_ref[...]) * g_ref[...]).astype(y_ref.dtype)


def _rwkv_out_call(o_f, o_b, prep, pw, tm_want=512):
    r, v, _, g, k2, _, _ = prep
    n, c = r.shape
    tm = _row_tile(n, tm_want)
    tile = pl.BlockSpec((tm, c), lambda i: (i, 0))
    tile2 = pl.BlockSpec((2, tm, c), lambda i: (0, i, 0))
    vec = pl.BlockSpec((1, c), lambda i: (0, 0))
    return pl.pallas_call(
        _rwkv_out_kernel,
        out_shape=jax.ShapeDtypeStruct((n, c), BF16),
        grid=(n // tm,),
        in_specs=[tile, tile, tile, tile2, tile, tile, vec, vec, vec,
                  pl.BlockSpec((c, c), lambda i: (0, 0))],
        out_specs=tile,
        compiler_params=_params("parallel"),
        name="rwkv_out",
    )(o_f, o_b, r, k2, v, g, pw["rk"], pw["gn_g"], pw["gn_b"], pw["bd"])


def _attn_prep_kernel(zw_ref, zd_ref, cos_ref, sin_ref, qn_ref, kn_ref,
                      qw_ref, kw_ref, vw_ref, qd_ref, kd_ref, vd_ref, *, rope):
    scale = HEAD_DIM ** -0.5
    hd = HEAD_DIM
    if rope:
        cosf = cos_ref[...]
        sins = sin_ref[...]
        lane = lax.broadcasted_iota(jnp.int32, cosf.shape, 1)
        first = (lane % (hd // 2)) < (hd // 4)

    def rot(xh):
        if not rope:
            return xh
        partner = jnp.where(first, pltpu.roll(xh, hd - hd // 4, 1), pltpu.roll(xh, hd // 4, 1))
        return xh * cosf + partner * sins

    def rms(xh, gain):
        return xh * lax.rsqrt(jnp.mean(xh * xh, axis=-1, keepdims=True) + RMS_EPS) * gain

    for h in range(Q_HEADS):
        sl = slice(h * hd, (h + 1) * hd)
        qw_ref[:, sl] = (rot(zw_ref[:, sl]) * scale).astype(BF16)
        qd_ref[:, sl] = (rot(rms(zd_ref[:, sl], qn_ref[...])) * scale).astype(BF16)
    for h in range(KV_HEADS):
        sl = slice(h * hd, (h + 1) * hd)
        ks = slice((Q_HEADS + h) * hd, (Q_HEADS + h + 1) * hd)
        vs = slice((Q_HEADS + KV_HEADS + h) * hd, (Q_HEADS + KV_HEADS + h + 1) * hd)
        kw_ref[:, sl] = rot(zw_ref[:, ks]).astype(BF16)
        kd_ref[:, sl] = rot(rms(zd_ref[:, ks], kn_ref[...])).astype(BF16)
        vw_ref[:, sl] = zw_ref[:, vs].astype(BF16)
        vd_ref[:, sl] = zd_ref[:, vs].astype(BF16)


def _attn_prep_call(zw, zd, t_len, cosf, sins, qn, kn, rope, tm_want=256):
    n = zw.shape[0]
    tm = _row_tile(t_len, tm_want)
    tps = t_len // tm
    hd = HEAD_DIM
    tile = lambda w: pl.BlockSpec((tm, w), lambda i: (i, 0))
    pos = pl.BlockSpec((tm, hd), lambda i: (i % tps, 0))
    vec = pl.BlockSpec((1, hd), lambda i: (0, 0))
    qs = jax.ShapeDtypeStruct((n, Q_HEADS * hd), BF16)
    ks = jax.ShapeDtypeStruct((n, KV_HEADS * hd), BF16)
    return pl.pallas_call(
        functools.partial(_attn_prep_kernel, rope=rope),
        out_shape=(qs, ks, ks, qs, ks, ks),
        grid=(n // tm,),
        in_specs=[tile(ATT_W), tile(ATT_W), pos, pos, vec, vec],
        out_specs=(tile(Q_HEADS * hd), tile(KV_HEADS * hd), tile(KV_HEADS * hd),
                   tile(Q_HEADS * hd), tile(KV_HEADS * hd), tile(KV_HEADS * hd)),
        compiler_params=_params("parallel"),
        name="attn_prep",
    )(zw, zd, cosf, sins, qn.reshape(1, hd), kn.reshape(1, hd))


def _full_attn_kernel(q_ref, k_ref, v_ref, *rest, sink):
    if sink:
        sink_ref, o_ref = rest
    else:
        (o_ref,) = rest
    hd = HEAD_DIM
    k = k_ref[...]
    v = v_ref[...]
    group = Q_HEADS // KV_HEADS
    for g in range(group):
        sl = slice(g * hd, (g + 1) * hd)
        s = _dot(q_ref[:, sl], k, _NT)
        m = jnp.max(s, axis=-1, keepdims=True)
        if sink:
            sk = sink_ref[:, g * hd:g * hd + 1]
            m = jnp.maximum(m, sk)
        p = jnp.exp(s - m)
        den = jnp.sum(p, axis=-1, keepdims=True)
        if sink:
            den = den + jnp.exp(sk - m)
        o_ref[:, sl] = (_dot(p.astype(BF16), v) / den).astype(o_ref.dtype)


def _full_attn_call(q, k, v, nbatch, sink_row=None, tq_want=256):
    n = q.shape[0]
    hd = HEAD_DIM
    tq_len = n // nbatch
    tk_len = k.shape[0] // nbatch
    tq = _row_tile(tq_len, tq_want)
    nq = tq_len // tq
    gw = (Q_HEADS // KV_HEADS) * hd
    in_specs = [pl.BlockSpec((tq, gw), lambda b, h, i: (b * nq + i, h)),
                pl.BlockSpec((tk_len, hd), lambda b, h, i: (b, h)),
                pl.BlockSpec((tk_len, hd), lambda b, h, i: (b, h))]
    args = [q, k, v]
    if sink_row is not None:
        in_specs.append(pl.BlockSpec((1, gw), lambda b, h, i: (0, h)))
        args.append(sink_row)
    return pl.pallas_call(
        functools.partial(_full_attn_kernel, sink=sink_row is not None),
        out_shape=jax.ShapeDtypeStruct((n, Q_HEADS * hd), BF16),
        grid=(nbatch, KV_HEADS, nq),
        in_specs=in_specs,
        out_specs=pl.BlockSpec((tq, gw), lambda b, h, i: (b * nq + i, h)),
        compiler_params=_params("parallel", "parallel", "arbitrary"),
        name="full_attn",
    )(*args)


def _win_attn_kernel(q_ref, kx_ref, vx_ref, kc_ref, vc_ref, sink_ref, o_ref):
    hd = HEAD_DIM
    nblk = pl.program_id(2)
    t_len = kx_ref.shape[0]
    span = 3 * BLOCK
    start = pl.multiple_of(jnp.clip((nblk - 1) * BLOCK, 0, t_len - span), BLOCK)
    kw = kx_ref[pl.ds(start, span), :]
    vw = vx_ref[pl.ds(start, span), :]
    kc = kc_ref[...]
    vc = vc_ref[...]
    qpos = nblk * BLOCK + lax.broadcasted_iota(jnp.int32, (BLOCK, span), 0)
    kpos = start + lax.broadcasted_iota(jnp.int32, (BLOCK, span), 1)
    allowed = jnp.abs(qpos - kpos) <= WIN
    group = Q_HEADS // KV_HEADS
    for g in range(group):
        sl = slice(g * hd, (g + 1) * hd)
        q = q_ref[:, sl]
        s_loc = jnp.where(allowed, _dot(q, kw, _NT), NEG_INF)
        s_ctx = _dot(q, kc, _NT)
        sk = sink_ref[:, g * hd:g * hd + 1]
        m = jnp.maximum(jnp.maximum(jnp.max(s_loc, axis=-1, keepdims=True),
                                    jnp.max(s_ctx, axis=-1, keepdims=True)), sk)
        p_loc = jnp.exp(s_loc - m)
        p_ctx = jnp.exp(s_ctx - m)
        den = (jnp.sum(p_loc, axis=-1, keepdims=True) + jnp.sum(p_ctx, axis=-1, keepdims=True)
               + jnp.exp(sk - m))
        o = _dot(p_ctx.astype(BF16), vc) + _dot(p_loc.astype(BF16), vw)
        o_ref[:, sl] = (o / den).astype(o_ref.dtype)


def _win_attn_call(q, kx, vx, kc, vc, sink_row, nbatch):
    n = q.shape[0]
    hd = HEAD_DIM
    t_len = n // nbatch
    l_len = kc.shape[0] // nbatch
    nb = t_len // BLOCK
    gw = (Q_HEADS // KV_HEADS) * hd
    return pl.pallas_call(
        _win_attn_kernel,
        out_shape=jax.ShapeDtypeStruct((n, Q_HEADS * hd), BF16),
        grid=(nbatch, KV_HEADS, nb),
        in_specs=[pl.BlockSpec((BLOCK, gw), lambda b, h, i: (b * nb + i, h)),
                  pl.BlockSpec((t_len, hd), lambda b, h, i: (b, h)),
                  pl.BlockSpec((t_len, hd), lambda b, h, i: (b, h)),
                  pl.BlockSpec((l_len, hd), lambda b, h, i: (b, h)),
                  pl.BlockSpec((l_len, hd), lambda b, h, i: (b, h)),
                  pl.BlockSpec((1, gw), lambda b, h, i: (0, h))],
        out_specs=pl.BlockSpec((BLOCK, gw), lambda b, h, i: (b * nb + i, h)),
        compiler_params=_params("parallel", "parallel", "arbitrary"),
        name="win_attn",
    )(q, kx, vx, kc, vc, sink_row)


def _merge_kernel(x_ref, gt_ref, b0_ref, b1_ref, b2_ref, b3_ref, gates_ref, wup_ref, wout_ref,
                  lg_ref, lb_ref, o_ref, acc_ref):
    kb = pl.program_id(1)
    for i, br in enumerate((b0_ref, b1_ref, b2_ref, b3_ref)):
        @pl.when(kb == i)
        def _(i=i, br=br):
            y = gates_ref[...] * _dot(br[...], wup_ref[0])
            if i == 0:
                acc_ref[...] = y
            else:
                acc_ref[...] += y

    @pl.when(kb == N_BRANCH - 1)
    def _():
        out = _dot(acc_ref[...].astype(BF16), wout_ref[...])
        y = DEEPNORM_ALPHA * x_ref[...] + gt_ref[0] * out
        o_ref[...] = _layer_norm(y, lg_ref[...], lb_ref[...])


def _merge_call(x, mod_rows, row_of_tile_fn, branches, gates, wup, wout, lg, lb, tm_want=512):
    n, d = x.shape
    c = BRANCH_W
    tm = _row_tile(n, tm_want)
    rot = functools.partial(row_of_tile_fn, tm)
    br_spec = pl.BlockSpec((tm, c), lambda i, j: (i, 0))
    return pl.pallas_call(
        _merge_kernel,
        out_shape=jax.ShapeDtypeStruct((n, d), F32),
        grid=(n // tm, N_BRANCH),
        in_specs=[pl.BlockSpec((tm, d), lambda i, j: (i, 0)),
                  _mod_spec(d, rot, 5),
                  br_spec, br_spec, br_spec, br_spec,
                  pl.BlockSpec((tm, d), lambda i, j: (i, j)),
                  pl.BlockSpec((1, c, d), lambda i, j: (j, 0, 0)),
                  pl.BlockSpec((d, d), lambda i, j: (0, 0)),
                  pl.BlockSpec((1, d), lambda i, j: (0, 0)),
                  pl.BlockSpec((1, d), lambda i, j: (0, 0))],
        out_specs=pl.BlockSpec((tm, d), lambda i, j: (i, 0)),
        scratch_shapes=[pltpu.VMEM((tm, d), F32)],
        compiler_params=_params("parallel", "arbitrary"),
        name="merge",
    )(x, mod_rows, *branches, gates, wup, wout, lg.reshape(1, d), lb.reshape(1, d))


def _pad_rows(w, rows):
    return jnp.pad(w, ((0, rows - w.shape[0]), (0, 0)))


def _rwkv_weights(w_rwkv, mu, w0, w2, a0, a2, g2, kk, ka, rk, gn_g, gn_b):
    c = BRANCH_W
    o = 3 * c
    cuts = [o, o + DECAY_LORA, o + 2 * DECAY_LORA, o + 2 * DECAY_LORA + ICLR_LORA,
            o + 2 * DECAY_LORA + 2 * ICLR_LORA]

    def pad_cols(m):
        parts = [m[:, :o]]
        for lo, hi in zip(cuts[:-1], cuts[1:]):
            parts.append(jnp.pad(m[:, lo:hi], ((0, 0), (0, LORA_PAD - (hi - lo)))))
        parts.append(m[:, cuts[-1]:])
        return jnp.concatenate(parts, axis=1)

    def block_diag(m):
        z = jnp.zeros((LORA_PAD, c), m.dtype)
        top = jnp.concatenate([_pad_rows(m[0], LORA_PAD), z], axis=1)
        bot = jnp.concatenate([z, _pad_rows(m[1], LORA_PAD)], axis=1)
        return jnp.concatenate([top, bot], axis=0)

    head = jnp.arange(c) // RWKV_HEAD
    return {
        "w": pad_cols(w_rwkv).astype(BF16),
        "mu": pad_cols(mu.reshape(1, -1)),
        "w0": w0.reshape(1, 2 * c), "w2": block_diag(w2).astype(BF16),
        "a0": a0.reshape(1, 2 * c), "a2": block_diag(a2).astype(BF16),
        "g2": g2.astype(BF16),
        "kk": kk.reshape(1, c), "ka": ka.reshape(1, c), "rk": rk.reshape(1, c),
        "gn_g": gn_g.reshape(1, c), "gn_b": gn_b.reshape(1, c),
        "bd": (head[:, None] == head[None, :]).astype(BF16),
    }


def _rope_tables(t_len):
    rows = t_len // GRID_W
    row = jnp.repeat(jnp.arange(rows), GRID_W).astype(F32)
    col = (jnp.arange(t_len) % GRID_W).astype(F32)
    n_freq = HEAD_DIM // 4
    inv = ROPE_THETA ** (-jnp.arange(n_freq, dtype=F32) / n_freq)
    ar = row[:, None] * inv
    ac = col[:, None] * inv
    cosf = jnp.concatenate([jnp.cos(ar), jnp.cos(ar), jnp.cos(ac), jnp.cos(ac)], axis=1)
    sins = jnp.concatenate([-jnp.sin(ar), jnp.sin(ar), -jnp.sin(ac), jnp.sin(ac)], axis=1)
    return cosf, sins


def kernel(x, c, ctx, c_ctx, w_mod, b_mod, ln_g, ln_b, ffn1_wi, ffn1_wo, ffn2_wi, ffn2_wo, w_in, b_gate, pool_w, pool_scale, rwkv_mu, rwkv_w0, rwkv_w2, rwkv_a0, rwkv_a2, rwkv_g2, rwkv_kk, rwkv_ka, rwkv_rk, rwkv_gn_g, rwkv_gn_b, c_sink, d_qnorm, d_knorm, w_up, w_out):
    nb, t_len, d = x.shape
    l_len = ctx.shape[1]
    depth = w_mod.shape[0]
    cw = BRANCH_W
    assert depth == DEPTH and t_len % GRID_W == 0 and t_len >= 3 * BLOCK

    mod_pad = 16
    c_all = jnp.zeros((mod_pad, d), F32).at[:nb].set(c).at[nb].set(c_ctx)
    cosf, sins = _rope_tables(t_len)
    s0 = jnp.zeros((2, nb, RWKV_HEAD, BRANCH_W), F32)

    def row_x(tm, i):
        return (i * tm) // t_len

    def row_c(tm, i):
        return nb

    xs = x.reshape(nb * t_len, d)
    xc = ctx.reshape(nb * l_len, d)
    o1 = cw
    o2 = o1 + 3 * cw + 2 * DECAY_LORA + 2 * ICLR_LORA + GATE_LORA
    o3 = o2 + ATT_W
    o4 = o3 + ATT_W

    for l in range(depth):
        last = l == depth - 1
        mod_rows = _mod_call(c_all, w_mod[l], b_mod[l]).reshape(mod_pad * N_MOD, 1, d)
        wi1, wo1 = ffn1_wi[l].astype(BF16), ffn1_wo[l].astype(BF16)
        wi2, wo2 = ffn2_wi[l].astype(BF16), ffn2_wo[l].astype(BF16)
        w_pool = w_in[l][:, :o1].astype(BF16)
        pw = _rwkv_weights(w_in[l][:, o1:o2], rwkv_mu[l], rwkv_w0[l], rwkv_w2[l], rwkv_a0[l],
                           rwkv_a2[l], rwkv_g2[l], rwkv_kk[l], rwkv_ka[l], rwkv_rk[l],
                           rwkv_gn_g[l], rwkv_gn_b[l])
        w_win = w_in[l][:, o2:o3].astype(BF16)
        w_dense = w_in[l][:, o3:o4].astype(BF16)
        w_gate = w_in[l][:, o4:].astype(BF16)
        pool_wb = pool_w[l].astype(BF16)
        wup = w_up[l].astype(BF16)
        wout = w_out[l].astype(BF16)
        sink_row = jnp.repeat(c_sink[l], HEAD_DIM).reshape(1, Q_HEADS * HEAD_DIM)

        xs = _ffn_call(xs, mod_rows, row_x, 0, wi1, wo1, ln_g[l, 0], ln_b[l, 0])
        xc = _ffn_call(xc, mod_rows, row_c, 0, wi1, wo1, ln_g[l, 0], ln_b[l, 0])

        proj_x = lambda w, bias=None: _proj_call(xs, mod_rows, row_x, 3, w, bias)
        proj_c = lambda w, bias=None: _proj_call(xc, mod_rows, row_c, 3, w, bias)

        prep_c = _rwkv_prep_call(proj_c(pw["w"]), l_len, pw)
        prep_x = _rwkv_prep_call(proj_x(pw["w"]), t_len, pw)
        oc_f, oc_b, s_ctx = _scan_call(prep_c, s0, nb, l_len)
        ox_f, ox_b, _ = _scan_call(prep_x, s_ctx, nb, t_len)
        yb_x = _rwkv_out_call(ox_f, ox_b, prep_x, pw)

        qw_c, kw_c, vw_c, qd_c, kd_c, vd_c = _attn_prep_call(
            proj_c(w_win), proj_c(w_dense), l_len, cosf, sins, d_qnorm[l], d_knorm[l], rope=False)
        qw_x, kw_x, vw_x, qd_x, kd_x, vd_x = _attn_prep_call(
            proj_x(w_win), proj_x(w_dense), t_len, cosf, sins, d_qnorm[l], d_knorm[l], rope=True)
        yc_x = _win_attn_call(qw_x, kw_x, vw_x, kw_c, vw_c, sink_row, nb)
        kv_w = KV_HEADS * HEAD_DIM
        kd_all = jnp.concatenate([kd_c.reshape(nb, l_len, kv_w), kd_x.reshape(nb, t_len, kv_w)],
                                 axis=1).reshape(nb * (l_len + t_len), kv_w)
        vd_all = jnp.concatenate([vd_c.reshape(nb, l_len, kv_w), vd_x.reshape(nb, t_len, kv_w)],
                                 axis=1).reshape(nb * (l_len + t_len), kv_w)
        yd_x = _full_attn_call(qd_x, kd_all, vd_all, nb)

        ya_x = _pool_call(proj_x(w_pool), nb, pool_wb, pool_scale[l])

        gates_x = proj_x(w_gate, b_gate[l])
        xs = _merge_call(xs, mod_rows, row_x, (ya_x, yb_x, yc_x, yd_x), gates_x, wup, wout,
                         ln_g[l, 1], ln_b[l, 1])
        xs = _ffn_call(xs, mod_rows, row_x, 6, wi2, wo2, ln_g[l, 2], ln_b[l, 2])

        if not last:
            ya_c = _pool_call(proj_c(w_pool), nb, pool_wb, pool_scale[l])
            yb_c = _rwkv_out_call(oc_f, oc_b, prep_c, pw)
            yc_c = _full_attn_call(qw_c, kw_c, vw_c, nb, sink_row)
            yd_c = _full_attn_call(qd_c, kd_c, vd_c, nb)
            gates_c = proj_c(w_gate, b_gate[l])
            xc = _merge_call(xc, mod_rows, row_c, (ya_c, yb_c, yc_c, yd_c), gates_c, wup, wout,
                             ln_g[l, 1], ln_b[l, 1])
            xc = _ffn_call(xc, mod_rows, row_c, 6, wi2, wo2, ln_g[l, 2], ln_b[l, 2])

    return xs.reshape(nb, t_len, d)
```

```python
import functools
import math

import jax
import jax.numpy as jnp
from jax import lax
from jax.experimental import pallas as pl
from jax.experimental.pallas import tpu as pltpu

F32 = jnp.float32
BF16 = jnp.bfloat16

GRID_W = 64
HEAD_DIM = 128
N_BRANCH = 4
BRANCH_W = 512
POOL_WINDOWS = (2, 4, 8, 16)
POOL_GROUP = BRANCH_W // len(POOL_WINDOWS)
RWKV_HEAD = 64
RWKV_HEADS = BRANCH_W // RWKV_HEAD
DECAY_LORA = 96
ICLR_LORA = 96
GATE_LORA = 256
LORA_PAD = 128
RWKV_W = 3 * BRANCH_W + 4 * LORA_PAD + GATE_LORA
WIN = 128
BLOCK = 128
Q_HEADS = 4
KV_HEADS = 2
ATT_W = (Q_HEADS + 2 * KV_HEADS) * HEAD_DIM
N_MOD = 9
DEPTH = 2
ROPE_THETA = 10000.0
DEEPNORM_ALPHA = (2 * DEPTH) ** 0.25
LN_EPS = 1e-6
RMS_EPS = 1e-6
GN_EPS = 64e-5
NEG_INF = -1e30
DECAY_SCALE = math.exp(-0.5)

V7X_VMEM_BYTES = 64 * 1024 * 1024
VMEM_LIMIT = V7X_VMEM_BYTES - 8 * 1024 * 1024
SCAN_CHUNK = 64

_NT = (((1,), (1,)), ((), ()))
_TN = (((0,), (0,)), ((), ()))


def _params(*sem):
    return pltpu.CompilerParams(dimension_semantics=sem, vmem_limit_bytes=VMEM_LIMIT)


def _dot(a, b, dims=None):
    if dims is None:
        return jnp.dot(a, b, preferred_element_type=F32)
    return lax.dot_general(a, b, dims, preferred_element_type=F32)


def _dot_exact_lhs(a_bf16, b):
    b0 = b.astype(BF16)
    r1 = b - b0.astype(F32)
    b1 = r1.astype(BF16)
    b2 = (r1 - b1.astype(F32)).astype(BF16)
    return _dot(a_bf16, b0) + (_dot(a_bf16, b1) + _dot(a_bf16, b2))


def _dot_exact_rhs(a, b_bf16):
    a0 = a.astype(BF16)
    r1 = a - a0.astype(F32)
    a1 = r1.astype(BF16)
    a2 = (r1 - a1.astype(F32)).astype(BF16)
    return _dot(a0, b_bf16) + (_dot(a1, b_bf16) + _dot(a2, b_bf16))


def _layer_norm(y, g, b):
    mu = jnp.mean(y, axis=-1, keepdims=True)
    yc = y - mu
    var = jnp.mean(yc * yc, axis=-1, keepdims=True)
    return yc * lax.rsqrt(var + LN_EPS) * g + b


def _run_chains(gens, width=None):
    started, pending = [], list(gens)
    width = width or len(pending)
    while started or pending:
        started += pending[:width]
        pending = pending[width:]
        alive = []
        for gen in started:
            try:
                next(gen)
                alive.append(gen)
            except StopIteration:
                pass
        started = alive


def _row_tile(n, want):
    t = min(want, n)
    assert n % t == 0, (n, t)
    return t


def _mod_kernel(c_ref, w_ref, b_ref, o_ref):
    c = c_ref[...]
    s = c * jax.nn.sigmoid(c)
    o_ref[...] = _dot(s.astype(BF16), w_ref[...].astype(BF16)) + b_ref[...]


def _mod_call(c_all, w, b):
    rows, d = c_all.shape
    nout = w.shape[1]
    tn = _row_tile(nout, min(d, 1024))
    return pl.pallas_call(
        _mod_kernel,
        out_shape=jax.ShapeDtypeStruct((rows, nout), F32),
        grid=(nout // tn,),
        in_specs=[pl.BlockSpec((rows, d), lambda j: (0, 0)),
                  pl.BlockSpec((d, tn), lambda j: (0, j)),
                  pl.BlockSpec((1, tn), lambda j: (0, j))],
        out_specs=pl.BlockSpec((rows, tn), lambda j: (0, j)),
        compiler_params=_params("arbitrary"),
        name="mod",
    )(c_all, w, b.reshape(1, nout))


def _mod_spec(d, row_of_tile, k):
    return pl.BlockSpec((1, 1, d), lambda i, j: (row_of_tile(i) * N_MOD + k, 0, 0))


def _ffn_kernel(x_ref, sh_ref, sc_ref, gt_ref, wg_ref, wu_ref, wo_ref, lg_ref, lb_ref,
                o_ref, xm_ref, acc_ref):
    j = pl.program_id(1)

    @pl.when(j == 0)
    def _():
        xm_ref[...] = (x_ref[...] * (1.0 + sc_ref[0]) + sh_ref[0]).astype(BF16)
        acc_ref[...] = jnp.zeros_like(acc_ref)

    xm = xm_ref[...]
    hg = _dot(xm, wg_ref[...])
    hu = _dot(xm, wu_ref[...])
    a = (hg * jax.nn.sigmoid(hg) * hu).astype(BF16)
    acc_ref[...] += _dot(a, wo_ref[...])

    @pl.when(j == pl.num_programs(1) - 1)
    def _():
        y = DEEPNORM_ALPHA * x_ref[...] + gt_ref[0] * (0.5 * acc_ref[...])
        o_ref[...] = _layer_norm(y, lg_ref[...], lb_ref[...])


def _ffn_call(x, mod_rows, row_of_tile_fn, k0, wi, wo, lg, lb, tm_want=512, tf_want=512):
    n, d = x.shape
    dff = wo.shape[0]
    tm = _row_tile(n, tm_want)
    tf = _row_tile(dff, tf_want)
    nj = dff // tf
    rot = functools.partial(row_of_tile_fn, tm)
    return pl.pallas_call(
        _ffn_kernel,
        out_shape=jax.ShapeDtypeStruct((n, d), F32),
        grid=(n // tm, nj),
        in_specs=[pl.BlockSpec((tm, d), lambda i, j: (i, 0)),
                  _mod_spec(d, rot, k0), _mod_spec(d, rot, k0 + 1), _mod_spec(d, rot, k0 + 2),
                  pl.BlockSpec((d, tf), lambda i, j: (0, j)),
                  pl.BlockSpec((d, tf), lambda i, j: (0, j + nj)),
                  pl.BlockSpec((tf, d), lambda i, j: (j, 0)),
                  pl.BlockSpec((1, d), lambda i, j: (0, 0)),
                  pl.BlockSpec((1, d), lambda i, j: (0, 0))],
        out_specs=pl.BlockSpec((tm, d), lambda i, j: (i, 0)),
        scratch_shapes=[pltpu.VMEM((tm, d), BF16), pltpu.VMEM((tm, d), F32)],
        compiler_params=_params("parallel", "arbitrary"),
        name="ffn",
    )(x, mod_rows, mod_rows, mod_rows, wi, wi, wo, lg.reshape(1, d), lb.reshape(1, d))


def _proj_kernel(x_ref, sh_ref, sc_ref, w_ref, *rest, gate):
    if gate:
        b_ref, o_ref, xm_ref = rest
    else:
        o_ref, xm_ref = rest

    @pl.when(pl.program_id(1) == 0)
    def _():
        xm_ref[...] = (x_ref[...] * (1.0 + sc_ref[0]) + sh_ref[0]).astype(BF16)

    z = _dot(xm_ref[...], w_ref[...])
    if gate:
        z = jax.nn.sigmoid(z + b_ref[...])
    o_ref[...] = z.astype(o_ref.dtype)


def _proj_call(x, mod_rows, row_of_tile_fn, k0, w, bias=None, tm_want=512, tn_want=1024):
    n, d = x.shape
    nout = w.shape[1]
    tm = _row_tile(n, tm_want)
    tn = nout if nout % tn_want else tn_want
    rot = functools.partial(row_of_tile_fn, tm)
    in_specs = [pl.BlockSpec((tm, d), lambda i, j: (i, 0)),
                _mod_spec(d, rot, k0), _mod_spec(d, rot, k0 + 1),
                pl.BlockSpec((d, tn), lambda i, j: (0, j))]
    args = [x, mod_rows, mod_rows, w]
    if bias is not None:
        in_specs.append(pl.BlockSpec((1, tn), lambda i, j: (0, j)))
        args.append(bias.reshape(1, nout))
    return pl.pallas_call(
        functools.partial(_proj_kernel, gate=bias is not None),
        out_shape=jax.ShapeDtypeStruct((n, nout), BF16 if bias is not None else F32),
        grid=(n // tm, nout // tn),
        in_specs=in_specs,
        out_specs=pl.BlockSpec((tm, tn), lambda i, j: (i, j)),
        scratch_shapes=[pltpu.VMEM((tm, d), BF16)],
        compiler_params=_params("parallel", "arbitrary"),
        name="proj",
    )(*args)


def _pool_kernel(z_ref, w_ref, s_ref, o_ref):
    g = pl.program_id(1)
    u = z_ref[...]
    t_len = u.shape[0]
    t = lax.broadcasted_iota(jnp.int32, u.shape, 0)
    for gi, win in enumerate(POOL_WINDOWS):
        @pl.when(g == gi)
        def _(win=win):
            half = win // 2
            acc = u
            for dlt in range(-half, half):
                if dlt == 0:
                    continue
                shifted = pltpu.roll(u, (-dlt) % t_len, 0)
                ok = (t + dlt >= 0) & (t + dlt < t_len)
                acc = acc + jnp.where(ok, shifted, 0.0)
            cnt = (jnp.minimum(t + half, t_len) - jnp.maximum(t - half, 0)).astype(F32)
            pooled = acc / cnt - u
            y = _dot(pooled.astype(BF16), w_ref[0]) * s_ref[...]
            o_ref[...] = y.astype(o_ref.dtype)


def _pool_call(z, nseq, w_bf16, scale):
    n, c = z.shape
    t_len = n // nseq
    ng = len(POOL_WINDOWS)
    return pl.pallas_call(
        _pool_kernel,
        out_shape=jax.ShapeDtypeStruct((n, c), BF16),
        grid=(nseq, ng),
        in_specs=[pl.BlockSpec((t_len, POOL_GROUP), lambda b, g: (b, g)),
                  pl.BlockSpec((1, POOL_GROUP, POOL_GROUP), lambda b, g: (g, 0, 0)),
                  pl.BlockSpec((1, POOL_GROUP), lambda b, g: (0, g))],
        out_specs=pl.BlockSpec((t_len, POOL_GROUP), lambda b, g: (b, g)),
        compiler_params=_params("parallel", "arbitrary"),
        name="pool",
    )(z, w_bf16, scale.reshape(1, c))


def _rwkv_prep_kernel(z_ref, zp_ref, zn_ref, mu_ref, w0_ref, w2_ref, a0_ref, a2_ref, g2_ref,
                      kkp_ref, ka_ref, bd_ref,
                      r_ref, v_ref, kk_ref, g_ref, k_ref, b_ref, lw_ref, *, tiles_per_seq):
    i = pl.program_id(0)
    cur = z_ref[...]
    tm = cur.shape[0]
    c = BRANCH_W
    row = lax.broadcasted_iota(jnp.int32, cur.shape, 0)
    pos = i % tiles_per_seq
    prev_edge = jnp.where(pos == 0, 0.0, zp_ref[7:8, :])
    next_edge = jnp.where(pos == tiles_per_seq - 1, 0.0, zn_ref[0:1, :])
    prev = jnp.where(row == 0, prev_edge, pltpu.roll(cur, 1, 0))
    nxt = jnp.where(row == tm - 1, next_edge, pltpu.roll(cur, tm - 1, 0))
    zs = cur + (0.5 * (prev + nxt) - cur) * mu_ref[...]

    r = zs[:, :c]
    k = zs[:, c:2 * c]
    v = zs[:, 2 * c:3 * c]
    o = 3 * c
    wd = zs[:, o:o + 2 * LORA_PAD]
    ad = zs[:, o + 2 * LORA_PAD:o + 4 * LORA_PAD]
    gd = zs[:, o + 4 * LORA_PAD:]
    z_w = w0_ref[...] + _dot(jnp.tanh(wd).astype(BF16), w2_ref[...])
    lw = -DECAY_SCALE * jax.nn.sigmoid(z_w)
    a = jax.nn.sigmoid(a0_ref[...] + _dot(ad.astype(BF16), a2_ref[...]))
    g = _dot(jax.nn.sigmoid(gd).astype(BF16), g2_ref[...])
    kk = k * kkp_ref[...]
    ssq = _dot_exact_rhs(kk * kk, bd_ref[...])
    kk = kk * lax.rsqrt(ssq + 1e-12)
    r_ref[...] = r
    v_ref[...] = v
    kk_ref[...] = kk
    g_ref[...] = g
    for dr in range(2):
        a_d = a[:, dr * c:(dr + 1) * c]
        k_ref[dr] = k * (1.0 + (a_d - 1.0) * ka_ref[...])
        b_ref[dr] = kk * a_d
        lw_ref[dr] = lw[:, dr * c:(dr + 1) * c]


def _rwkv_prep_call(z, t_len, pw, tm_want=256):
    n, w = z.shape
    c = BRANCH_W
    tm = _row_tile(t_len, tm_want)
    tps = t_len // tm
    nblk8 = n // 8
    full = lambda shape: pl.BlockSpec(shape, lambda i: (0,) * len(shape))
    tile = pl.BlockSpec((tm, c), lambda i: (i, 0))
    tile2 = pl.BlockSpec((2, tm, c), lambda i: (0, i, 0))
    one = jax.ShapeDtypeStruct((n, c), F32)
    two = jax.ShapeDtypeStruct((2, n, c), F32)
    return pl.pallas_call(
        functools.partial(_rwkv_prep_kernel, tiles_per_seq=tps),
        out_shape=(one, one, one, one, two, two, two),
        grid=(n // tm,),
        in_specs=[pl.BlockSpec((tm, w), lambda i: (i, 0)),
                  pl.BlockSpec((8, w), lambda i: (jnp.maximum(i * (tm // 8) - 1, 0), 0)),
                  pl.BlockSpec((8, w), lambda i: (jnp.minimum((i + 1) * (tm // 8), nblk8 - 1), 0)),
                  full((1, w)), full((1, 2 * c)), full((2 * LORA_PAD, 2 * c)),
                  full((1, 2 * c)), full((2 * LORA_PAD, 2 * c)), full((GATE_LORA, c)),
                  full((1, c)), full((1, c)), full((c, c))],
        out_specs=(tile, tile, tile, tile, tile2, tile2, tile2),
        compiler_params=_params("parallel"),
        name="rwkv_prep",
    )(z, z, z, pw["mu"], pw["w0"], pw["w2"], pw["a0"], pw["a2"], pw["g2"],
      pw["kk"], pw["ka"], pw["bd"])


def _block_diag4(x_bf16, head_mask):
    return jnp.concatenate([x_bf16] * 4, axis=0) * head_mask


def _dot_bd(a, x, head_mask):
    return _dot(a.astype(BF16), _block_diag4(x.astype(BF16), head_mask))


def _scan_masks(rev, cl):
    hw = RWKV_HEAD
    gw = 4 * hw
    row = lax.broadcasted_iota(jnp.int32, (cl, gw), 0)
    lane = lax.broadcasted_iota(jnp.int32, (cl, gw), 1)
    pos_t = (cl - 1 - row) if rev else row
    pos_s = lane % hw
    pos_s = (cl - 1 - pos_s) if rev else pos_s
    nlev = cl.bit_length() - 1
    brow = lax.broadcasted_iota(jnp.int32, (gw, gw), 0) // hw
    blane = lax.broadcasted_iota(jnp.int32, (gw, gw), 1) // hw
    ti = lax.broadcasted_iota(jnp.int32, (cl, cl), 0)
    si = lax.broadcasted_iota(jnp.int32, (cl, cl), 1)
    return {
        "lane_head": lane // hw,
        "strict": pos_s < pos_t,
        "incl": pos_s <= pos_t,
        "eye": jnp.where(pos_s == pos_t, 1.0, 0.0),
        "levels": [((pos_t >> (lev + 1)) == (pos_s >> (lev + 1)))
                   & (((pos_t >> lev) & 1) == 1) & (((pos_s >> lev) & 1) == 0)
                   for lev in range(nlev)],
        "head": jnp.where(brow == blane, 1.0, 0.0).astype(BF16),
        "tri": jnp.where((si >= ti) if rev else (si <= ti), 1.0, 0.0).astype(BF16),
    }


def _scan_group(mk, ab, qb, kb, bb, kh, bh, vb, g_tot, s_ref, sl, out):
    cl = ab.shape[0]
    hw = RWKV_HEAD
    hm = mk["head"]
    aq = jnp.concatenate([ab, qb], axis=0)
    p_b = _dot(aq, _block_diag4(bb, hm), _NT)
    p_k = _dot(aq, _block_diag4(kb, hm), _NT)
    n_ab = jnp.where(mk["strict"], p_b[:cl], 0.0)
    a_ak = jnp.where(mk["strict"], p_k[:cl], 0.0).astype(BF16)
    a_qb = jnp.where(mk["incl"], p_b[cl:], 0.0).astype(BF16)
    a_qk = jnp.where(mk["incl"], p_k[cl:], 0.0).astype(BF16)
    yield
    t_inv = mk["eye"] - jnp.where(mk["levels"][0], n_ab, 0.0)
    for lev in range(1, len(mk["levels"])):
        x = _dot_bd(jnp.where(mk["levels"][lev], n_ab, 0.0), t_inv, hm)
        yield
        t_inv = t_inv - _dot_bd(t_inv, x, hm)
        yield
    s_g = s_ref[:, sl]
    p_s = _dot(aq, _block_diag4(s_g.astype(BF16), hm), _NT)
    p_v = _dot(jnp.concatenate([a_ak, a_qk], axis=0), _block_diag4(vb, hm))
    yield
    ub = (-_dot_bd(t_inv, p_s[:cl] + p_v[:cl], hm)).astype(BF16)
    yield
    out.append(p_s[cl:] + p_v[cl:] + _dot(a_qb, _block_diag4(ub, hm)))
    full = _dot(jnp.concatenate([ub, vb], axis=0), jnp.concatenate([bh, kh], axis=0), _TN)
    upd = jnp.where(mk["lane_head"] == 0, full[:hw], 0.0)
    for h in range(1, 4):
        upd = upd + jnp.where(mk["lane_head"] == h, full[h * hw:(h + 1) * hw], 0.0)
    s_ref[:, sl] = s_g * g_tot + upd


def _scan_direction(mk, r, v, kk, k, b, lw, s_ref, out):
    cl, c = lw.shape
    gw = 4 * RWKV_HEAD
    assert cl == RWKV_HEAD
    cum = _dot_exact_lhs(mk["tri"], lw)
    tot = jnp.sum(lw, axis=0, keepdims=True)
    g_inv = jnp.exp(-cum)
    g_end = jnp.exp(tot - cum)
    g_tot = jnp.exp(tot)
    qb = (r * jnp.exp(cum)).astype(BF16)
    ab = (kk * jnp.exp(cum - lw)).astype(BF16)
    kb = (k * g_inv).astype(BF16)
    bb = (b * g_inv).astype(BF16)
    kh = (k * g_end).astype(BF16)
    bh = (b * g_end).astype(BF16)
    vb = v.astype(BF16)
    gens = []
    for g in range(c // gw):
        sl = slice(g * gw, (g + 1) * gw)
        gens.append(_scan_group(mk, ab[:, sl], qb[:, sl], kb[:, sl], bb[:, sl], kh[:, sl], bh[:, sl],
                                vb[:, sl], g_tot[:, sl], s_ref, sl, out))
    return gens


SCAN_ROWS_PER_STEP = 4


def _scan_kernel(rf_ref, rb_ref, vf_ref, vb_ref, kkf_ref, kkb_ref, kf_ref, kb_ref, bf_ref, bb_ref,
                 lwf_ref, lwb_ref, s0_ref, of_ref, ob_ref, sfin_ref, s_ref):
    n = pl.program_id(1)

    @pl.when(n == 0)
    def _():
        s_ref[...] = s0_ref[...]

    nrow, cl, _ = rf_ref.shape
    mk_f = _scan_masks(False, cl)
    mk_b = _scan_masks(True, cl)
    gens, outs = [], []
    for i in range(nrow):
        out_f, out_b = [], []
        outs.append((out_f, out_b))
        gens += _scan_direction(mk_f, rf_ref[i], vf_ref[i], kkf_ref[i], kf_ref[i], bf_ref[i],
                                lwf_ref[i], s_ref.at[0, i], out_f)
        gens += _scan_direction(mk_b, rb_ref[i], vb_ref[i], kkb_ref[i], kb_ref[i], bb_ref[i],
                                lwb_ref[i], s_ref.at[1, i], out_b)
    _run_chains(gens)
    for i, (out_f, out_b) in enumerate(outs):
        of_ref[i] = jnp.concatenate(out_f, axis=1)
        ob_ref[i] = jnp.concatenate(out_b, axis=1)

    @pl.when(n == pl.num_programs(1) - 1)
    def _():
        sfin_ref[...] = s_ref[...]


def _scan_call(prep, s0, nbatch, t_len):
    r, v, kk, _, k2, b2, lw2 = prep
    n, c = r.shape
    cl = SCAN_CHUNK
    nc = t_len // cl
    nrow = math.gcd(nbatch, SCAN_ROWS_PER_STEP)
    seq = lambda a: a.reshape(a.shape[:-2] + (nbatch, t_len, c))
    fwd = pl.BlockSpec((nrow, cl, c), lambda bi, i: (bi, i, 0))
    bwd = pl.BlockSpec((nrow, cl, c), lambda bi, i: (bi, nc - 1 - i, 0))
    fwd2 = pl.BlockSpec((None, nrow, cl, c), lambda bi, i: (0, bi, i, 0))
    bwd2 = pl.BlockSpec((None, nrow, cl, c), lambda bi, i: (1, bi, nc - 1 - i, 0))
    state = pl.BlockSpec((2, nrow, RWKV_HEAD, c), lambda bi, i: (0, bi, 0, 0))
    o_f, o_b, s_fin = pl.pallas_call(
        _scan_kernel,
        out_shape=(jax.ShapeDtypeStruct((nbatch, t_len, c), F32),
                   jax.ShapeDtypeStruct((nbatch, t_len, c), F32),
                   jax.ShapeDtypeStruct(s0.shape, F32)),
        grid=(nbatch // nrow, nc),
        in_specs=[fwd, bwd, fwd, bwd, fwd, bwd, fwd2, bwd2, fwd2, bwd2, fwd2, bwd2, state],
        out_specs=(fwd, bwd, state),
        scratch_shapes=[pltpu.VMEM((2, nrow, RWKV_HEAD, c), F32)],
        compiler_params=_params("parallel", "arbitrary"),
        name="rwkv_scan",
    )(seq(r), seq(r), seq(v), seq(v), seq(kk), seq(kk), seq(k2), seq(k2), seq(b2), seq(b2),
      seq(lw2), seq(lw2), s0)
    return o_f.reshape(n, c), o_b.reshape(n, c), s_fin


def _rwkv_out_kernel(of_ref, ob_ref, r_ref, k_ref, v_ref, g_ref, rk_ref, gg_ref, gb_ref, bd_ref, y_ref):
    bd = bd_ref[...]
    inv_n = 1.0 / RWKV_HEAD
    s = of_ref[...] + ob_ref[...]
    mu = _dot_exact_rhs(s, bd) * inv_n
    sc = s - mu
    var = _dot_exact_rhs(sc * sc, bd) * inv_n
    y = sc * lax.rsqrt(var + GN_EPS) * gg_ref[...] + gb_ref[...]
    r = r_ref[...]
    bonus = _dot_exact_rhs(r * k_ref[0] * rk_ref[...] + r * k_ref[1] * rk_ref[...], bd)
    y_ref[...] = ((y + bonus * v_ref[...]) * g_ref[...]).astype(y_ref.dtype)


def _rwkv_out_call(o_f, o_b, prep, pw, tm_want=512):
    r, v, _, g, k2, _, _ = prep
    n, c = r.shape
    tm = _row_tile(n, tm_want)
    tile = pl.BlockSpec((tm, c), lambda i: (i, 0))
    tile2 = pl.BlockSpec((2, tm, c), lambda i: (0, i, 0))
    vec = pl.BlockSpec((1, c), lambda i: (0, 0))
    return pl.pallas_call(
        _rwkv_out_kernel,
        out_shape=jax.ShapeDtypeStruct((n, c), BF16),
        grid=(n // tm,),
        in_specs=[tile, tile, tile, tile2, tile, tile, vec, vec, vec,
                  pl.BlockSpec((c, c), lambda i: (0, 0))],
        out_specs=tile,
        compiler_params=_params("parallel"),
        name="rwkv_out",
    )(o_f, o_b, r, k2, v, g, pw["rk"], pw["gn_g"], pw["gn_b"], pw["bd"])


def _attn_prep_kernel(zw_ref, zd_ref, cos_ref, sin_ref, qn_ref, kn_ref,
                      qw_ref, kw_ref, vw_ref, qd_ref, kd_ref, vd_ref, *, rope):
    scale = HEAD_DIM ** -0.5
    hd = HEAD_DIM
    if rope:
        cosf = cos_ref[...]
        sins = sin_ref[...]
        lane = lax.broadcasted_iota(jnp.int32, cosf.shape, 1)
        first = (lane % (hd // 2)) < (hd // 4)

    def rot(xh):
        if not rope:
            return xh
        partner = jnp.where(first, pltpu.roll(xh, hd - hd // 4, 1), pltpu.roll(xh, hd // 4, 1))
        return xh * cosf + partner * sins

    def rms(xh, gain):
        return xh * lax.rsqrt(jnp.mean(xh * xh, axis=-1, keepdims=True) + RMS_EPS) * gain

    for h in range(Q_HEADS):
        sl = slice(h * hd, (h + 1) * hd)
        qw_ref[:, sl] = (rot(zw_ref[:, sl]) * scale).astype(BF16)
        qd_ref[:, sl] = (rot(rms(zd_ref[:, sl], qn_ref[...])) * scale).astype(BF16)
    for h in range(KV_HEADS):
        sl = slice(h * hd, (h + 1) * hd)
        ks = slice((Q_HEADS + h) * hd, (Q_HEADS + h + 1) * hd)
        vs = slice((Q_HEADS + KV_HEADS + h) * hd, (Q_HEADS + KV_HEADS + h + 1) * hd)
        kw_ref[:, sl] = rot(zw_ref[:, ks]).astype(BF16)
        kd_ref[:, sl] = rot(rms(zd_ref[:, ks], kn_ref[...])).astype(BF16)
        vw_ref[:, sl] = zw_ref[:, vs].astype(BF16)
        vd_ref[:, sl] = zd_ref[:, vs].astype(BF16)


def _attn_prep_call(zw, zd, t_len, cosf, sins, qn, kn, rope, tm_want=256):
    n = zw.shape[0]
    tm = _row_tile(t_len, tm_want)
    tps = t_len // tm
    hd = HEAD_DIM
    tile = lambda w: pl.BlockSpec((tm, w), lambda i: (i, 0))
    pos = pl.BlockSpec((tm, hd), lambda i: (i % tps, 0))
    vec = pl.BlockSpec((1, hd), lambda i: (0, 0))
    qs = jax.ShapeDtypeStruct((n, Q_HEADS * hd), BF16)
    ks = jax.ShapeDtypeStruct((n, KV_HEADS * hd), BF16)
    return pl.pallas_call(
        functools.partial(_attn_prep_kernel, rope=rope),
        out_shape=(qs, ks, ks, qs, ks, ks),
        grid=(n // tm,),
        in_specs=[tile(ATT_W), tile(ATT_W), pos, pos, vec, vec],
        out_specs=(tile(Q_HEADS * hd), tile(KV_HEADS * hd), tile(KV_HEADS * hd),
                   tile(Q_HEADS * hd), tile(KV_HEADS * hd), tile(KV_HEADS * hd)),
        compiler_params=_params("parallel"),
        name="attn_prep",
    )(zw, zd, cosf, sins, qn.reshape(1, hd), kn.reshape(1, hd))


def _full_attn_kernel(q_ref, k_ref, v_ref, *rest, sink):
    if sink:
        sink_ref, o_ref = rest
    else:
        (o_ref,) = rest
    hd = HEAD_DIM
    k = k_ref[...]
    v = v_ref[...]
    group = Q_HEADS // KV_HEADS
    tq = q_ref.shape[0]
    rows = min(tq, FULL_ATTN_CHAIN_ROWS)

    def chain(g, r0):
        sl = slice(g * hd, (g + 1) * hd)
        rs = slice(r0, r0 + rows)
        s = _dot(q_ref[rs, sl], k, _NT)
        yield
        m = jnp.max(s, axis=-1, keepdims=True)
        if sink:
            sk = sink_ref[:, g * hd:g * hd + 1]
            m = jnp.maximum(m, sk)
        p = jnp.exp(s - m)
        den = jnp.sum(p, axis=-1, keepdims=True)
        if sink:
            den = den + jnp.exp(sk - m)
        yield
        o_ref[rs, sl] = (_dot(p.astype(BF16), v) / den).astype(o_ref.dtype)

    _run_chains([chain(g, r0) for g in range(group) for r0 in range(0, tq, rows)], 1)


FULL_ATTN_CHAIN_ROWS = 128


def _full_attn_call(q, k, v, nbatch, sink_row=None, tq_want=512):
    n = q.shape[0]
    hd = HEAD_DIM
    tq_len = n // nbatch
    tk_len = k.shape[0] // nbatch
    tq = _row_tile(tq_len, tq_want)
    nq = tq_len // tq
    gw = (Q_HEADS // KV_HEADS) * hd
    in_specs = [pl.BlockSpec((tq, gw), lambda b, h, i: (b * nq + i, h)),
                pl.BlockSpec((tk_len, hd), lambda b, h, i: (b, h)),
                pl.BlockSpec((tk_len, hd), lambda b, h, i: (b, h))]
    args = [q, k, v]
    if sink_row is not None:
        in_specs.append(pl.BlockSpec((1, gw), lambda b, h, i: (0, h)))
        args.append(sink_row)
    return pl.pallas_call(
        functools.partial(_full_attn_kernel, sink=sink_row is not None),
        out_shape=jax.ShapeDtypeStruct((n, Q_HEADS * hd), BF16),
        grid=(nbatch, KV_HEADS, nq),
        in_specs=in_specs,
        out_specs=pl.BlockSpec((tq, gw), lambda b, h, i: (b * nq + i, h)),
        compiler_params=_params("parallel", "parallel", "arbitrary"),
        name="full_attn",
    )(*args)


WIN_BLOCKS_PER_STEP = 8


def _win_attn_kernel(q_ref, kx_ref, vx_ref, kc_ref, vc_ref, sink_ref, o_ref):
    hd = HEAD_DIM
    step = pl.program_id(2)
    t_len = kx_ref.shape[0]
    nsub = q_ref.shape[0] // BLOCK
    span = 3 * BLOCK
    kc = kc_ref[...]
    vc = vc_ref[...]
    l_len = kc.shape[0]
    group = Q_HEADS // KV_HEADS

    def block_keys(j):
        nblk = step * nsub + j
        start = pl.multiple_of(jnp.clip((nblk - 1) * BLOCK, 0, t_len - span), BLOCK)
        k_all = jnp.concatenate([kc, kx_ref[pl.ds(start, span), :]], axis=0)
        v_all = jnp.concatenate([vc, vx_ref[pl.ds(start, span), :]], axis=0)
        col = lax.broadcasted_iota(jnp.int32, (BLOCK, l_len + span), 1)
        qpos = nblk * BLOCK + lax.broadcasted_iota(jnp.int32, (BLOCK, l_len + span), 0)
        allowed = (col < l_len) | (jnp.abs(qpos - (start + col - l_len)) <= WIN)
        return k_all, v_all, allowed

    def chain(j, g, keys):
        k_all, v_all, allowed = keys
        rs = slice(j * BLOCK, (j + 1) * BLOCK)
        sl = slice(g * hd, (g + 1) * hd)
        s = _dot(q_ref[rs, sl], k_all, _NT)
        yield
        s = jnp.where(allowed, s, NEG_INF)
        sk = sink_ref[:, g * hd:g * hd + 1]
        m = jnp.maximum(jnp.max(s, axis=-1, keepdims=True), sk)
        p = jnp.exp(s - m)
        den = jnp.sum(p, axis=-1, keepdims=True) + jnp.exp(sk - m)
        yield
        o_ref[rs, sl] = (_dot(p.astype(BF16), v_all) / den).astype(o_ref.dtype)

    chains = []
    for j in range(nsub):
        keys = block_keys(j)
        chains += [chain(j, g, keys) for g in range(group)]
    _run_chains(chains, 4)


def _win_attn_call(q, kx, vx, kc, vc, sink_row, nbatch):
    n = q.shape[0]
    hd = HEAD_DIM
    t_len = n // nbatch
    l_len = kc.shape[0] // nbatch
    tq = _row_tile(t_len, WIN_BLOCKS_PER_STEP * BLOCK)
    nq = t_len // tq
    gw = (Q_HEADS // KV_HEADS) * hd
    return pl.pallas_call(
        _win_attn_kernel,
        out_shape=jax.ShapeDtypeStruct((n, Q_HEADS * hd), BF16),
        grid=(nbatch, KV_HEADS, nq),
        in_specs=[pl.BlockSpec((tq, gw), lambda b, h, i: (b * nq + i, h)),
                  pl.BlockSpec((t_len, hd), lambda b, h, i: (b, h)),
                  pl.BlockSpec((t_len, hd), lambda b, h, i: (b, h)),
                  pl.BlockSpec((l_len, hd), lambda b, h, i: (b, h)),
                  pl.BlockSpec((l_len, hd), lambda b, h, i: (b, h)),
                  pl.BlockSpec((1, gw), lambda b, h, i: (0, h))],
        out_specs=pl.BlockSpec((tq, gw), lambda b, h, i: (b * nq + i, h)),
        compiler_params=_params("parallel", "parallel", "arbitrary"),
        name="win_attn",
    )(q, kx, vx, kc, vc, sink_row)


def _merge_kernel(x_ref, gt_ref, b0_ref, b1_ref, b2_ref, b3_ref, gates_ref, wup_ref, wout_ref,
                  lg_ref, lb_ref, o_ref, acc_ref):
    kb = pl.program_id(1)
    for i, br in enumerate((b0_ref, b1_ref, b2_ref, b3_ref)):
        @pl.when(kb == i)
        def _(i=i, br=br):
            y = gates_ref[...] * _dot(br[...], wup_ref[0])
            if i == 0:
                acc_ref[...] = y
            else:
                acc_ref[...] += y

    @pl.when(kb == N_BRANCH - 1)
    def _():
        out = _dot(acc_ref[...].astype(BF16), wout_ref[...])
        y = DEEPNORM_ALPHA * x_ref[...] + gt_ref[0] * out
        o_ref[...] = _layer_norm(y, lg_ref[...], lb_ref[...])


def _merge_call(x, mod_rows, row_of_tile_fn, branches, gates, wup, wout, lg, lb, tm_want=512):
    n, d = x.shape
    c = BRANCH_W
    tm = _row_tile(n, tm_want)
    rot = functools.partial(row_of_tile_fn, tm)
    br_spec = pl.BlockSpec((tm, c), lambda i, j: (i, 0))
    return pl.pallas_call(
        _merge_kernel,
        out_shape=jax.ShapeDtypeStruct((n, d), F32),
        grid=(n // tm, N_BRANCH),
        in_specs=[pl.BlockSpec((tm, d), lambda i, j: (i, 0)),
                  _mod_spec(d, rot, 5),
                  br_spec, br_spec, br_spec, br_spec,
                  pl.BlockSpec((tm, d), lambda i, j: (i, j)),
                  pl.BlockSpec((1, c, d), lambda i, j: (j, 0, 0)),
                  pl.BlockSpec((d, d), lambda i, j: (0, 0)),
                  pl.BlockSpec((1, d), lambda i, j: (0, 0)),
                  pl.BlockSpec((1, d), lambda i, j: (0, 0))],
        out_specs=pl.BlockSpec((tm, d), lambda i, j: (i, 0)),
        scratch_shapes=[pltpu.VMEM((tm, d), F32)],
        compiler_params=_params("parallel", "arbitrary"),
        name="merge",
    )(x, mod_rows, *branches, gates, wup, wout, lg.reshape(1, d), lb.reshape(1, d))


def _pad_rows(w, rows):
    return jnp.pad(w, ((0, rows - w.shape[0]), (0, 0)))


def _rwkv_weights(w_rwkv, mu, w0, w2, a0, a2, g2, kk, ka, rk, gn_g, gn_b):
    c = BRANCH_W
    o = 3 * c
    cuts = [o, o + DECAY_LORA, o + 2 * DECAY_LORA, o + 2 * DECAY_LORA + ICLR_LORA,
            o + 2 * DECAY_LORA + 2 * ICLR_LORA]

    def pad_cols(m):
        parts = [m[:, :o]]
        for lo, hi in zip(cuts[:-1], cuts[1:]):
            parts.append(jnp.pad(m[:, lo:hi], ((0, 0), (0, LORA_PAD - (hi - lo)))))
        parts.append(m[:, cuts[-1]:])
        return jnp.concatenate(parts, axis=1)

    def block_diag(m):
        z = jnp.zeros((LORA_PAD, c), m.dtype)
        top = jnp.concatenate([_pad_rows(m[0], LORA_PAD), z], axis=1)
        bot = jnp.concatenate([z, _pad_rows(m[1], LORA_PAD)], axis=1)
        return jnp.concatenate([top, bot], axis=0)

    head = jnp.arange(c) // RWKV_HEAD
    return {
        "w": pad_cols(w_rwkv).astype(BF16),
        "mu": pad_cols(mu.reshape(1, -1)),
        "w0": w0.reshape(1, 2 * c), "w2": block_diag(w2).astype(BF16),
        "a0": a0.reshape(1, 2 * c), "a2": block_diag(a2).astype(BF16),
        "g2": g2.astype(BF16),
        "kk": kk.reshape(1, c), "ka": ka.reshape(1, c), "rk": rk.reshape(1, c),
        "gn_g": gn_g.reshape(1, c), "gn_b": gn_b.reshape(1, c),
        "bd": (head[:, None] == head[None, :]).astype(BF16),
    }


def _rope_tables(t_len):
    rows = t_len // GRID_W
    row = jnp.repeat(jnp.arange(rows), GRID_W).astype(F32)
    col = (jnp.arange(t_len) % GRID_W).astype(F32)
    n_freq = HEAD_DIM // 4
    inv = ROPE_THETA ** (-jnp.arange(n_freq, dtype=F32) / n_freq)
    ar = row[:, None] * inv
    ac = col[:, None] * inv
    cosf = jnp.concatenate([jnp.cos(ar), jnp.cos(ar), jnp.cos(ac), jnp.cos(ac)], axis=1)
    sins = jnp.concatenate([-jnp.sin(ar), jnp.sin(ar), -jnp.sin(ac), jnp.sin(ac)], axis=1)
    return cosf, sins


def kernel(x, c, ctx, c_ctx, w_mod, b_mod, ln_g, ln_b, ffn1_wi, ffn1_wo, ffn2_wi, ffn2_wo, w_in, b_gate, pool_w, pool_scale, rwkv_mu, rwkv_w0, rwkv_w2, rwkv_a0, rwkv_a2, rwkv_g2, rwkv_kk, rwkv_ka, rwkv_rk, rwkv_gn_g, rwkv_gn_b, c_sink, d_qnorm, d_knorm, w_up, w_out):
    nb, t_len, d = x.shape
    l_len = ctx.shape[1]
    depth = w_mod.shape[0]
    cw = BRANCH_W
    assert depth == DEPTH and t_len % GRID_W == 0 and t_len >= 3 * BLOCK

    mod_pad = 16
    c_all = jnp.zeros((mod_pad, d), F32).at[:nb].set(c).at[nb].set(c_ctx)
    cosf, sins = _rope_tables(t_len)
    s0 = jnp.zeros((2, nb, RWKV_HEAD, BRANCH_W), F32)

    def row_x(tm, i):
        return (i * tm) // t_len

    def row_c(tm, i):
        return nb

    xs = x.reshape(nb * t_len, d)
    xc = ctx.reshape(nb * l_len, d)
    o1 = cw
    o2 = o1 + 3 * cw + 2 * DECAY_LORA + 2 * ICLR_LORA + GATE_LORA
    o3 = o2 + ATT_W
    o4 = o3 + ATT_W

    for l in range(depth):
        last = l == depth - 1
        mod_rows = _mod_call(c_all, w_mod[l], b_mod[l]).reshape(mod_pad * N_MOD, 1, d)
        wi1, wo1 = ffn1_wi[l].astype(BF16), ffn1_wo[l].astype(BF16)
        wi2, wo2 = ffn2_wi[l].astype(BF16), ffn2_wo[l].astype(BF16)
        w_pool = w_in[l][:, :o1].astype(BF16)
        pw = _rwkv_weights(w_in[l][:, o1:o2], rwkv_mu[l], rwkv_w0[l], rwkv_w2[l], rwkv_a0[l],
                           rwkv_a2[l], rwkv_g2[l], rwkv_kk[l], rwkv_ka[l], rwkv_rk[l],
                           rwkv_gn_g[l], rwkv_gn_b[l])
        w_win = w_in[l][:, o2:o3].astype(BF16)
        w_dense = w_in[l][:, o3:o4].astype(BF16)
        w_gate = w_in[l][:, o4:].astype(BF16)
        pool_wb = pool_w[l].astype(BF16)
        wup = w_up[l].astype(BF16)
        wout = w_out[l].astype(BF16)
        sink_row = jnp.repeat(c_sink[l], HEAD_DIM).reshape(1, Q_HEADS * HEAD_DIM)

        xs = _ffn_call(xs, mod_rows, row_x, 0, wi1, wo1, ln_g[l, 0], ln_b[l, 0])
        xc = _ffn_call(xc, mod_rows, row_c, 0, wi1, wo1, ln_g[l, 0], ln_b[l, 0])

        proj_x = lambda w, bias=None: _proj_call(xs, mod_rows, row_x, 3, w, bias)
        proj_c = lambda w, bias=None: _proj_call(xc, mod_rows, row_c, 3, w, bias)

        prep_c = _rwkv_prep_call(proj_c(pw["w"]), l_len, pw)
        prep_x = _rwkv_prep_call(proj_x(pw["w"]), t_len, pw)
        oc_f, oc_b, s_ctx = _scan_call(prep_c, s0, nb, l_len)
        ox_f, ox_b, _ = _scan_call(prep_x, s_ctx, nb, t_len)
        yb_x = _rwkv_out_call(ox_f, ox_b, prep_x, pw)

        qw_c, kw_c, vw_c, qd_c, kd_c, vd_c = _attn_prep_call(
            proj_c(w_win), proj_c(w_dense), l_len, cosf, sins, d_qnorm[l], d_knorm[l], rope=False)
        qw_x, kw_x, vw_x, qd_x, kd_x, vd_x = _attn_prep_call(
            proj_x(w_win), proj_x(w_dense), t_len, cosf, sins, d_qnorm[l], d_knorm[l], rope=True)
        yc_x = _win_attn_call(qw_x, kw_x, vw_x, kw_c, vw_c, sink_row, nb)
        kv_w = KV_HEADS * HEAD_DIM
        kd_all = jnp.concatenate([kd_c.reshape(nb, l_len, kv_w), kd_x.reshape(nb, t_len, kv_w)],
                                 axis=1).reshape(nb * (l_len + t_len), kv_w)
        vd_all = jnp.concatenate([vd_c.reshape(nb, l_len, kv_w), vd_x.reshape(nb, t_len, kv_w)],
                                 axis=1).reshape(nb * (l_len + t_len), kv_w)
        yd_x = _full_attn_call(qd_x, kd_all, vd_all, nb)

        ya_x = _pool_call(proj_x(w_pool), nb, pool_wb, pool_scale[l])

        gates_x = proj_x(w_gate, b_gate[l])
        xs = _merge_call(xs, mod_rows, row_x, (ya_x, yb_x, yc_x, yd_x), gates_x, wup, wout,
                         ln_g[l, 1], ln_b[l, 1])
        xs = _ffn_call(xs, mod_rows, row_x, 6, wi2, wo2, ln_g[l, 2], ln_b[l, 2])

        if not last:
            ya_c = _pool_call(proj_c(w_pool), nb, pool_wb, pool_scale[l])
            yb_c = _rwkv_out_call(oc_f, oc_b, prep_c, pw)
            yc_c = _full_attn_call(qw_c, kw_c, vw_c, nb, sink_row)
            yd_c = _full_attn_call(qd_c, kd_c, vd_c, nb)
            gates_c = proj_c(w_gate, b_gate[l])
            xc = _merge_call(xc, mod_rows, row_c, (ya_c, yb_c, yc_c, yd_c), gates_c, wup, wout,
                             ln_g[l, 1], ln_b[l, 1])
            xc = _ffn_call(xc, mod_rows, row_c, 6, wi2, wo2, ln_g[l, 2], ln_b[l, 2])

    return xs.reshape(nb, t_len, d)
```

```python
import functools
import math

import jax
import jax.numpy as jnp
from jax import lax
from jax.experimental import pallas as pl
from jax.experimental.pallas import tpu as pltpu

F32 = jnp.float32
BF16 = jnp.bfloat16

GRID_W = 64
HEAD_DIM = 128
N_BRANCH = 4
BRANCH_W = 512
POOL_WINDOWS = (2, 4, 8, 16)
POOL_GROUP = BRANCH_W // len(POOL_WINDOWS)
RWKV_HEAD = 64
RWKV_HEADS = BRANCH_W // RWKV_HEAD
DECAY_LORA = 96
ICLR_LORA = 96
GATE_LORA = 256
LORA_PAD = 128
RWKV_W = 3 * BRANCH_W + 4 * LORA_PAD + GATE_LORA
WIN = 128
BLOCK = 128
Q_HEADS = 4
KV_HEADS = 2
ATT_W = (Q_HEADS + 2 * KV_HEADS) * HEAD_DIM
N_MOD = 9
DEPTH = 2
ROPE_THETA = 10000.0
DEEPNORM_ALPHA = (2 * DEPTH) ** 0.25
LN_EPS = 1e-6
RMS_EPS = 1e-6
GN_EPS = 64e-5
NEG_INF = -1e30
DECAY_SCALE = math.exp(-0.5)

V7X_VMEM_BYTES = 64 * 1024 * 1024
VMEM_LIMIT = V7X_VMEM_BYTES - 8 * 1024 * 1024
SCAN_CHUNK = 64

_NT = (((1,), (1,)), ((), ()))
_TN = (((0,), (0,)), ((), ()))


def _params(*sem):
    return pltpu.CompilerParams(dimension_semantics=sem, vmem_limit_bytes=VMEM_LIMIT)


def _dot(a, b, dims=None):
    if dims is None:
        return jnp.dot(a, b, preferred_element_type=F32)
    return lax.dot_general(a, b, dims, preferred_element_type=F32)


def _dot_exact_lhs(a_bf16, b):
    b0 = b.astype(BF16)
    r1 = b - b0.astype(F32)
    b1 = r1.astype(BF16)
    b2 = (r1 - b1.astype(F32)).astype(BF16)
    return _dot(a_bf16, b0) + (_dot(a_bf16, b1) + _dot(a_bf16, b2))


def _dot_exact_rhs(a, b_bf16):
    a0 = a.astype(BF16)
    r1 = a - a0.astype(F32)
    a1 = r1.astype(BF16)
    a2 = (r1 - a1.astype(F32)).astype(BF16)
    return _dot(a0, b_bf16) + (_dot(a1, b_bf16) + _dot(a2, b_bf16))


def _layer_norm(y, g, b):
    mu = jnp.mean(y, axis=-1, keepdims=True)
    yc = y - mu
    var = jnp.mean(yc * yc, axis=-1, keepdims=True)
    return yc * lax.rsqrt(var + LN_EPS) * g + b


def _run_chains(gens, width=None):
    started, pending = [], list(gens)
    width = width or len(pending)
    while started or pending:
        started += pending[:width]
        pending = pending[width:]
        alive = []
        for gen in started:
            try:
                next(gen)
                alive.append(gen)
            except StopIteration:
                pass
        started = alive


def _row_tile(n, want):
    t = min(want, n)
    assert n % t == 0, (n, t)
    return t


def _mod_kernel(c_ref, w_ref, b_ref, o_ref):
    c = c_ref[...]
    s = c * jax.nn.sigmoid(c)
    o_ref[...] = _dot(s.astype(BF16), w_ref[...].astype(BF16)) + b_ref[...]


def _mod_call(c_all, w, b):
    rows, d = c_all.shape
    nout = w.shape[1]
    tn = _row_tile(nout, min(d, 1024))
    return pl.pallas_call(
        _mod_kernel,
        out_shape=jax.ShapeDtypeStruct((rows, nout), F32),
        grid=(nout // tn,),
        in_specs=[pl.BlockSpec((rows, d), lambda j: (0, 0)),
                  pl.BlockSpec((d, tn), lambda j: (0, j)),
                  pl.BlockSpec((1, tn), lambda j: (0, j))],
        out_specs=pl.BlockSpec((rows, tn), lambda j: (0, j)),
        compiler_params=_params("arbitrary"),
        name="mod",
    )(c_all, w, b.reshape(1, nout))


def _mod_spec(d, row_of_tile, k):
    return pl.BlockSpec((1, 1, d), lambda i, j: (row_of_tile(i) * N_MOD + k, 0, 0))


def _ffn_kernel(x_ref, sh_ref, sc_ref, gt_ref, wg_ref, wu_ref, wo_ref, lg_ref, lb_ref,
                o_ref, xm_ref, acc_ref, *, nj):
    j = pl.program_id(1)
    tm = x_ref.shape[0]

    def chain(rs, first, final):
        if first:
            xm = (x_ref[rs, :] * (1.0 + sc_ref[0]) + sh_ref[0]).astype(BF16)
            xm_ref[rs, :] = xm
        else:
            xm = xm_ref[rs, :]
        hg = _dot(xm, wg_ref[...])
        hu = _dot(xm, wu_ref[...])
        yield
        y = _dot((hg * jax.nn.sigmoid(hg) * hu).astype(BF16), wo_ref[...])
        yield
        if final:
            if not first:
                y = acc_ref[rs, :] + y
            z = DEEPNORM_ALPHA * x_ref[rs, :] + gt_ref[0] * (0.5 * y)
            o_ref[rs, :] = _layer_norm(z, lg_ref[...], lb_ref[...])
        elif first:
            acc_ref[rs, :] = y
        else:
            acc_ref[rs, :] += y

    def body(first, final):
        split = (first or final) and tm % 16 == 0
        rows = tm // 2 if split else tm
        _run_chains([chain(slice(r0, r0 + rows), first, final) for r0 in range(0, tm, rows)], 1)

    if nj == 1:
        body(True, True)
    else:
        pl.when(j == 0)(functools.partial(body, True, False))
        pl.when(j == nj - 1)(functools.partial(body, False, True))
        if nj > 2:
            pl.when((j > 0) & (j < nj - 1))(functools.partial(body, False, False))


def _ffn_call(x, mod_rows, row_of_tile_fn, k0, wi, wo, lg, lb, tm_want=512, tf_want=512):
    n, d = x.shape
    dff = wo.shape[0]
    tm = _row_tile(n, tm_want)
    tf = _row_tile(dff, tf_want)
    nj = dff // tf
    rot = functools.partial(row_of_tile_fn, tm)
    return pl.pallas_call(
        functools.partial(_ffn_kernel, nj=nj),
        out_shape=jax.ShapeDtypeStruct((n, d), F32),
        grid=(n // tm, nj),
        in_specs=[pl.BlockSpec((tm, d), lambda i, j: (i, 0)),
                  _mod_spec(d, rot, k0), _mod_spec(d, rot, k0 + 1), _mod_spec(d, rot, k0 + 2),
                  pl.BlockSpec((d, tf), lambda i, j: (0, j)),
                  pl.BlockSpec((d, tf), lambda i, j: (0, j + nj)),
                  pl.BlockSpec((tf, d), lambda i, j: (j, 0)),
                  pl.BlockSpec((1, d), lambda i, j: (0, 0)),
                  pl.BlockSpec((1, d), lambda i, j: (0, 0))],
        out_specs=pl.BlockSpec((tm, d), lambda i, j: (i, 0)),
        scratch_shapes=[pltpu.VMEM((tm, d), BF16), pltpu.VMEM((tm, d), F32)],
        compiler_params=_params("parallel", "arbitrary"),
        name="ffn",
    )(x, mod_rows, mod_rows, mod_rows, wi, wi, wo, lg.reshape(1, d), lb.reshape(1, d))


def _proj_kernel(x_ref, sh_ref, sc_ref, w_ref, o_ref, xm_ref):
    @pl.when(pl.program_id(1) == 0)
    def _():
        xm_ref[...] = (x_ref[...] * (1.0 + sc_ref[0]) + sh_ref[0]).astype(BF16)

    o_ref[...] = _dot(xm_ref[...], w_ref[...])


def _proj_call(x, mod_rows, row_of_tile_fn, k0, w, tm_want=512, tn_want=1024):
    n, d = x.shape
    nout = w.shape[1]
    tm = _row_tile(n, tm_want)
    tn = nout if nout % tn_want else tn_want
    rot = functools.partial(row_of_tile_fn, tm)
    return pl.pallas_call(
        _proj_kernel,
        out_shape=jax.ShapeDtypeStruct((n, nout), F32),
        grid=(n // tm, nout // tn),
        in_specs=[pl.BlockSpec((tm, d), lambda i, j: (i, 0)),
                  _mod_spec(d, rot, k0), _mod_spec(d, rot, k0 + 1),
                  pl.BlockSpec((d, tn), lambda i, j: (0, j))],
        out_specs=pl.BlockSpec((tm, tn), lambda i, j: (i, j)),
        scratch_shapes=[pltpu.VMEM((tm, d), BF16)],
        compiler_params=_params("parallel", "arbitrary"),
        name="proj",
    )(x, mod_rows, mod_rows, w)


def _pool_kernel(z_ref, w_ref, s_ref, o_ref):
    g = pl.program_id(1)
    u = z_ref[...]
    t_len = u.shape[0]
    t = lax.broadcasted_iota(jnp.int32, u.shape, 0)
    for gi, win in enumerate(POOL_WINDOWS):
        @pl.when(g == gi)
        def _(win=win):
            half = win // 2
            acc = u
            for dlt in range(-half, half):
                if dlt == 0:
                    continue
                shifted = pltpu.roll(u, (-dlt) % t_len, 0)
                ok = (t + dlt >= 0) & (t + dlt < t_len)
                acc = acc + jnp.where(ok, shifted, 0.0)
            cnt = (jnp.minimum(t + half, t_len) - jnp.maximum(t - half, 0)).astype(F32)
            pooled = acc / cnt - u
            y = _dot(pooled.astype(BF16), w_ref[0]) * s_ref[...]
            o_ref[...] = y.astype(o_ref.dtype)


def _pool_call(z, nseq, w_bf16, scale):
    n, c = z.shape
    t_len = n // nseq
    ng = len(POOL_WINDOWS)
    return pl.pallas_call(
        _pool_kernel,
        out_shape=jax.ShapeDtypeStruct((n, c), BF16),
        grid=(nseq, ng),
        in_specs=[pl.BlockSpec((t_len, POOL_GROUP), lambda b, g: (b, g)),
                  pl.BlockSpec((1, POOL_GROUP, POOL_GROUP), lambda b, g: (g, 0, 0)),
                  pl.BlockSpec((1, POOL_GROUP), lambda b, g: (0, g))],
        out_specs=pl.BlockSpec((t_len, POOL_GROUP), lambda b, g: (b, g)),
        compiler_params=_params("parallel", "arbitrary"),
        name="pool",
    )(z, w_bf16, scale.reshape(1, c))


def _rwkv_prep_kernel(z_ref, zp_ref, zn_ref, mu_ref, w0_ref, w2_ref, a0_ref, a2_ref, g2_ref,
                      kkp_ref, ka_ref, bd_ref,
                      r_ref, v_ref, kk_ref, g_ref, k_ref, b_ref, lw_ref, *, tiles_per_seq):
    i = pl.program_id(0)
    cur = z_ref[...]
    tm = cur.shape[0]
    c = BRANCH_W
    row = lax.broadcasted_iota(jnp.int32, cur.shape, 0)
    pos = i % tiles_per_seq
    prev_edge = jnp.where(pos == 0, 0.0, zp_ref[7:8, :])
    next_edge = jnp.where(pos == tiles_per_seq - 1, 0.0, zn_ref[0:1, :])
    prev = jnp.where(row == 0, prev_edge, pltpu.roll(cur, 1, 0))
    nxt = jnp.where(row == tm - 1, next_edge, pltpu.roll(cur, tm - 1, 0))
    zs = cur + (0.5 * (prev + nxt) - cur) * mu_ref[...]

    r = zs[:, :c]
    k = zs[:, c:2 * c]
    v = zs[:, 2 * c:3 * c]
    o = 3 * c
    wd = zs[:, o:o + 2 * LORA_PAD]
    ad = zs[:, o + 2 * LORA_PAD:o + 4 * LORA_PAD]
    gd = zs[:, o + 4 * LORA_PAD:]
    z_w = w0_ref[...] + _dot(jnp.tanh(wd).astype(BF16), w2_ref[...])
    lw = -DECAY_SCALE * jax.nn.sigmoid(z_w)
    a = jax.nn.sigmoid(a0_ref[...] + _dot(ad.astype(BF16), a2_ref[...]))
    g = _dot(jax.nn.sigmoid(gd).astype(BF16), g2_ref[...])
    kk = k * kkp_ref[...]
    ssq = _dot_exact_rhs(kk * kk, bd_ref[...])
    kk = kk * lax.rsqrt(ssq + 1e-12)
    r_ref[...] = r
    v_ref[...] = v
    kk_ref[...] = kk
    g_ref[...] = g
    for dr in range(2):
        a_d = a[:, dr * c:(dr + 1) * c]
        k_ref[dr] = k * (1.0 + (a_d - 1.0) * ka_ref[...])
        b_ref[dr] = kk * a_d
        lw_ref[dr] = lw[:, dr * c:(dr + 1) * c]


def _rwkv_prep_call(z, t_len, pw, tm_want=256):
    n, w = z.shape
    c = BRANCH_W
    tm = _row_tile(t_len, tm_want)
    tps = t_len // tm
    nblk8 = n // 8
    full = lambda shape: pl.BlockSpec(shape, lambda i: (0,) * len(shape))
    tile = pl.BlockSpec((tm, c), lambda i: (i, 0))
    tile2 = pl.BlockSpec((2, tm, c), lambda i: (0, i, 0))
    one = jax.ShapeDtypeStruct((n, c), F32)
    two = jax.ShapeDtypeStruct((2, n, c), F32)
    return pl.pallas_call(
        functools.partial(_rwkv_prep_kernel, tiles_per_seq=tps),
        out_shape=(one, one, one, one, two, two, two),
        grid=(n // tm,),
        in_specs=[pl.BlockSpec((tm, w), lambda i: (i, 0)),
                  pl.BlockSpec((8, w), lambda i: (jnp.maximum(i * (tm // 8) - 1, 0), 0)),
                  pl.BlockSpec((8, w), lambda i: (jnp.minimum((i + 1) * (tm // 8), nblk8 - 1), 0)),
                  full((1, w)), full((1, 2 * c)), full((2 * LORA_PAD, 2 * c)),
                  full((1, 2 * c)), full((2 * LORA_PAD, 2 * c)), full((GATE_LORA, c)),
                  full((1, c)), full((1, c)), full((c, c))],
        out_specs=(tile, tile, tile, tile, tile2, tile2, tile2),
        compiler_params=_params("parallel"),
        name="rwkv_prep",
    )(z, z, z, pw["mu"], pw["w0"], pw["w2"], pw["a0"], pw["a2"], pw["g2"],
      pw["kk"], pw["ka"], pw["bd"])


def _block_diag4(x_bf16, head_mask):
    return jnp.concatenate([x_bf16] * 4, axis=0) * head_mask


def _dot_bd(a, x, head_mask):
    return _dot(a.astype(BF16), _block_diag4(x.astype(BF16), head_mask))


def _scan_masks(rev, cl):
    hw = RWKV_HEAD
    gw = 4 * hw
    row = lax.broadcasted_iota(jnp.int32, (cl, gw), 0)
    lane = lax.broadcasted_iota(jnp.int32, (cl, gw), 1)
    pos_t = (cl - 1 - row) if rev else row
    pos_s = lane % hw
    pos_s = (cl - 1 - pos_s) if rev else pos_s
    nlev = cl.bit_length() - 1
    brow = lax.broadcasted_iota(jnp.int32, (gw, gw), 0) // hw
    blane = lax.broadcasted_iota(jnp.int32, (gw, gw), 1) // hw
    ti = lax.broadcasted_iota(jnp.int32, (cl, cl), 0)
    si = lax.broadcasted_iota(jnp.int32, (cl, cl), 1)
    return {
        "lane_head": lane // hw,
        "strict": pos_s < pos_t,
        "incl": pos_s <= pos_t,
        "eye": jnp.where(pos_s == pos_t, 1.0, 0.0),
        "levels": [((pos_t >> (lev + 1)) == (pos_s >> (lev + 1)))
                   & (((pos_t >> lev) & 1) == 1) & (((pos_s >> lev) & 1) == 0)
                   for lev in range(nlev)],
        "head": jnp.where(brow == blane, 1.0, 0.0).astype(BF16),
        "tri": jnp.where((si >= ti) if rev else (si <= ti), 1.0, 0.0).astype(BF16),
    }


def _scan_group(mk, ab, qb, kb, bb, kh, bh, vb, g_tot, s_ref, sl, out):
    cl = ab.shape[0]
    hw = RWKV_HEAD
    hm = mk["head"]
    aq = jnp.concatenate([ab, qb], axis=0)
    p_b = _dot(aq, _block_diag4(bb, hm), _NT)
    p_k = _dot(aq, _block_diag4(kb, hm), _NT)
    n_ab = jnp.where(mk["strict"], p_b[:cl], 0.0)
    a_ak = jnp.where(mk["strict"], p_k[:cl], 0.0).astype(BF16)
    a_qb = jnp.where(mk["incl"], p_b[cl:], 0.0).astype(BF16)
    a_qk = jnp.where(mk["incl"], p_k[cl:], 0.0).astype(BF16)
    yield
    t_inv = mk["eye"] - jnp.where(mk["levels"][0], n_ab, 0.0)
    for lev in range(1, len(mk["levels"])):
        x = _dot_bd(jnp.where(mk["levels"][lev], n_ab, 0.0), t_inv, hm)
        yield
        t_inv = t_inv - _dot_bd(t_inv, x, hm)
        yield
    s_g = s_ref[:, sl]
    p_s = _dot(aq, _block_diag4(s_g.astype(BF16), hm), _NT)
    p_v = _dot(jnp.concatenate([a_ak, a_qk], axis=0), _block_diag4(vb, hm))
    yield
    ub = (-_dot_bd(t_inv, p_s[:cl] + p_v[:cl], hm)).astype(BF16)
    yield
    out.append(p_s[cl:] + p_v[cl:] + _dot(a_qb, _block_diag4(ub, hm)))
    full = _dot(jnp.concatenate([ub, vb], axis=0), jnp.concatenate([bh, kh], axis=0), _TN)
    upd = jnp.where(mk["lane_head"] == 0, full[:hw], 0.0)
    for h in range(1, 4):
        upd = upd + jnp.where(mk["lane_head"] == h, full[h * hw:(h + 1) * hw], 0.0)
    s_ref[:, sl] = s_g * g_tot + upd


def _scan_direction(mk, r, v, kk, k, b, lw, s_ref, out):
    cl, c = lw.shape
    gw = 4 * RWKV_HEAD
    assert cl == RWKV_HEAD
    cum = _dot_exact_lhs(mk["tri"], lw)
    tot = jnp.sum(lw, axis=0, keepdims=True)
    g_inv = jnp.exp(-cum)
    g_end = jnp.exp(tot - cum)
    g_tot = jnp.exp(tot)
    qb = (r * jnp.exp(cum)).astype(BF16)
    ab = (kk * jnp.exp(cum - lw)).astype(BF16)
    kb = (k * g_inv).astype(BF16)
    bb = (b * g_inv).astype(BF16)
    kh = (k * g_end).astype(BF16)
    bh = (b * g_end).astype(BF16)
    vb = v.astype(BF16)
    gens = []
    for g in range(c // gw):
        sl = slice(g * gw, (g + 1) * gw)
        gens.append(_scan_group(mk, ab[:, sl], qb[:, sl], kb[:, sl], bb[:, sl], kh[:, sl], bh[:, sl],
                                vb[:, sl], g_tot[:, sl], s_ref, sl, out))
    return gens


SCAN_ROWS_PER_STEP = 4


def _scan_kernel(rf_ref, rb_ref, vf_ref, vb_ref, kkf_ref, kkb_ref, kf_ref, kb_ref, bf_ref, bb_ref,
                 lwf_ref, lwb_ref, s0_ref, of_ref, ob_ref, sfin_ref, s_ref):
    n = pl.program_id(1)

    @pl.when(n == 0)
    def _():
        s_ref[...] = s0_ref[...]

    nrow, cl, _ = rf_ref.shape
    mk_f = _scan_masks(False, cl)
    mk_b = _scan_masks(True, cl)
    gens, outs = [], []
    for i in range(nrow):
        out_f, out_b = [], []
        outs.append((out_f, out_b))
        gens += _scan_direction(mk_f, rf_ref[i], vf_ref[i], kkf_ref[i], kf_ref[i], bf_ref[i],
                                lwf_ref[i], s_ref.at[0, i], out_f)
        gens += _scan_direction(mk_b, rb_ref[i], vb_ref[i], kkb_ref[i], kb_ref[i], bb_ref[i],
                                lwb_ref[i], s_ref.at[1, i], out_b)
    _run_chains(gens)
    for i, (out_f, out_b) in enumerate(outs):
        of_ref[i] = jnp.concatenate(out_f, axis=1)
        ob_ref[i] = jnp.concatenate(out_b, axis=1)

    @pl.when(n == pl.num_programs(1) - 1)
    def _():
        sfin_ref[...] = s_ref[...]


def _scan_call(prep, s0, nbatch, t_len):
    r, v, kk, _, k2, b2, lw2 = prep
    n, c = r.shape
    cl = SCAN_CHUNK
    nc = t_len // cl
    nrow = math.gcd(nbatch, SCAN_ROWS_PER_STEP)
    seq = lambda a: a.reshape(a.shape[:-2] + (nbatch, t_len, c))
    fwd = pl.BlockSpec((nrow, cl, c), lambda bi, i: (bi, i, 0))
    bwd = pl.BlockSpec((nrow, cl, c), lambda bi, i: (bi, nc - 1 - i, 0))
    fwd2 = pl.BlockSpec((None, nrow, cl, c), lambda bi, i: (0, bi, i, 0))
    bwd2 = pl.BlockSpec((None, nrow, cl, c), lambda bi, i: (1, bi, nc - 1 - i, 0))
    state = pl.BlockSpec((2, nrow, RWKV_HEAD, c), lambda bi, i: (0, bi, 0, 0))
    o_f, o_b, s_fin = pl.pallas_call(
        _scan_kernel,
        out_shape=(jax.ShapeDtypeStruct((nbatch, t_len, c), F32),
                   jax.ShapeDtypeStruct((nbatch, t_len, c), F32),
                   jax.ShapeDtypeStruct(s0.shape, F32)),
        grid=(nbatch // nrow, nc),
        in_specs=[fwd, bwd, fwd, bwd, fwd, bwd, fwd2, bwd2, fwd2, bwd2, fwd2, bwd2, state],
        out_specs=(fwd, bwd, state),
        scratch_shapes=[pltpu.VMEM((2, nrow, RWKV_HEAD, c), F32)],
        compiler_params=_params("parallel", "arbitrary"),
        name="rwkv_scan",
    )(seq(r), seq(r), seq(v), seq(v), seq(kk), seq(kk), seq(k2), seq(k2), seq(b2), seq(b2),
      seq(lw2), seq(lw2), s0)
    return o_f.reshape(n, c), o_b.reshape(n, c), s_fin


def _rwkv_out_kernel(of_ref, ob_ref, r_ref, k_ref, v_ref, g_ref, rk_ref, gg_ref, gb_ref, bd_ref, y_ref):
    bd = bd_ref[...]
    inv_n = 1.0 / RWKV_HEAD
    s = of_ref[...] + ob_ref[...]
    mu = _dot_exact_rhs(s, bd) * inv_n
    sc = s - mu
    var = _dot_exact_rhs(sc * sc, bd) * inv_n
    y = sc * lax.rsqrt(var + GN_EPS) * gg_ref[...] + gb_ref[...]
    r = r_ref[...]
    bonus = _dot_exact_rhs(r * k_ref[0] * rk_ref[...] + r * k_ref[1] * rk_ref[...], bd)
    y_ref[...] = ((y + bonus * v_ref[...]) * g_ref[...]).astype(y_ref.dtype)


def _rwkv_out_call(o_f, o_b, prep, pw, tm_want=512):
    r, v, _, g, k2, _, _ = prep
    n, c = r.shape
    tm = _row_tile(n, tm_want)
    tile = pl.BlockSpec((tm, c), lambda i: (i, 0))
    tile2 = pl.BlockSpec((2, tm, c), lambda i: (0, i, 0))
    vec = pl.BlockSpec((1, c), lambda i: (0, 0))
    return pl.pallas_call(
        _rwkv_out_kernel,
        out_shape=jax.ShapeDtypeStruct((n, c), BF16),
        grid=(n // tm,),
        in_specs=[tile, tile, tile, tile2, tile, tile, vec, vec, vec,
                  pl.BlockSpec((c, c), lambda i: (0, 0))],
        out_specs=tile,
        compiler_params=_params("parallel"),
        name="rwkv_out",
    )(o_f, o_b, r, k2, v, g, pw["rk"], pw["gn_g"], pw["gn_b"], pw["bd"])


def _attn_prep_kernel(zw_ref, zd_ref, cos_ref, sin_ref, qn_ref, kn_ref,
                      qw_ref, kw_ref, vw_ref, qd_ref, kd_ref, vd_ref, *, rope):
    scale = HEAD_DIM ** -0.5
    hd = HEAD_DIM
    if rope:
        cosf = cos_ref[...]
        sins = sin_ref[...]
        lane = lax.broadcasted_iota(jnp.int32, cosf.shape, 1)
        first = (lane % (hd // 2)) < (hd // 4)

    def rot(xh):
        if not rope:
            return xh
        partner = jnp.where(first, pltpu.roll(xh, hd - hd // 4, 1), pltpu.roll(xh, hd // 4, 1))
        return xh * cosf + partner * sins

    def rms(xh, gain):
        return xh * lax.rsqrt(jnp.mean(xh * xh, axis=-1, keepdims=True) + RMS_EPS) * gain

    for h in range(Q_HEADS):
        sl = slice(h * hd, (h + 1) * hd)
        qw_ref[:, sl] = (rot(zw_ref[:, sl]) * scale).astype(BF16)
        qd_ref[:, sl] = (rot(rms(zd_ref[:, sl], qn_ref[...])) * scale).astype(BF16)
    for h in range(KV_HEADS):
        sl = slice(h * hd, (h + 1) * hd)
        ks = slice((Q_HEADS + h) * hd, (Q_HEADS + h + 1) * hd)
        vs = slice((Q_HEADS + KV_HEADS + h) * hd, (Q_HEADS + KV_HEADS + h + 1) * hd)
        kw_ref[:, sl] = rot(zw_ref[:, ks]).astype(BF16)
        kd_ref[:, sl] = rot(rms(zd_ref[:, ks], kn_ref[...])).astype(BF16)
        vw_ref[:, sl] = zw_ref[:, vs].astype(BF16)
        vd_ref[:, sl] = zd_ref[:, vs].astype(BF16)


def _attn_prep_call(zw, zd, t_len, cosf, sins, qn, kn, rope, tm_want=256):
    n = zw.shape[0]
    tm = _row_tile(t_len, tm_want)
    tps = t_len // tm
    hd = HEAD_DIM
    tile = lambda w: pl.BlockSpec((tm, w), lambda i: (i, 0))
    pos = pl.BlockSpec((tm, hd), lambda i: (i % tps, 0))
    vec = pl.BlockSpec((1, hd), lambda i: (0, 0))
    qs = jax.ShapeDtypeStruct((n, Q_HEADS * hd), BF16)
    ks = jax.ShapeDtypeStruct((n, KV_HEADS * hd), BF16)
    return pl.pallas_call(
        functools.partial(_attn_prep_kernel, rope=rope),
        out_shape=(qs, ks, ks, qs, ks, ks),
        grid=(n // tm,),
        in_specs=[tile(ATT_W), tile(ATT_W), pos, pos, vec, vec],
        out_specs=(tile(Q_HEADS * hd), tile(KV_HEADS * hd), tile(KV_HEADS * hd),
                   tile(Q_HEADS * hd), tile(KV_HEADS * hd), tile(KV_HEADS * hd)),
        compiler_params=_params("parallel"),
        name="attn_prep",
    )(zw, zd, cosf, sins, qn.reshape(1, hd), kn.reshape(1, hd))


def _full_attn_kernel(q_ref, k_ref, v_ref, *rest, sink):
    if sink:
        sink_ref, o_ref = rest
    else:
        (o_ref,) = rest
    hd = HEAD_DIM
    k = k_ref[...]
    v = v_ref[...]
    group = Q_HEADS // KV_HEADS
    tq = q_ref.shape[0]
    rows = min(tq, FULL_ATTN_CHAIN_ROWS)

    def chain(g, r0):
        sl = slice(g * hd, (g + 1) * hd)
        rs = slice(r0, r0 + rows)
        s = _dot(q_ref[rs, sl], k, _NT)
        yield
        m = jnp.max(s, axis=-1, keepdims=True)
        if sink:
            sk = sink_ref[:, g * hd:g * hd + 1]
            m = jnp.maximum(m, sk)
        p = jnp.exp(s - m)
        den = jnp.sum(p, axis=-1, keepdims=True)
        if sink:
            den = den + jnp.exp(sk - m)
        yield
        o_ref[rs, sl] = (_dot(p.astype(BF16), v) / den).astype(o_ref.dtype)

    _run_chains([chain(g, r0) for g in range(group) for r0 in range(0, tq, rows)], 1)


FULL_ATTN_CHAIN_ROWS = 128


def _full_attn_call(q, k, v, nbatch, sink_row=None, tq_want=512):
    n = q.shape[0]
    hd = HEAD_DIM
    tq_len = n // nbatch
    tk_len = k.shape[0] // nbatch
    tq = _row_tile(tq_len, tq_want)
    nq = tq_len // tq
    gw = (Q_HEADS // KV_HEADS) * hd
    in_specs = [pl.BlockSpec((tq, gw), lambda b, h, i: (b * nq + i, h)),
                pl.BlockSpec((tk_len, hd), lambda b, h, i: (b, h)),
                pl.BlockSpec((tk_len, hd), lambda b, h, i: (b, h))]
    args = [q, k, v]
    if sink_row is not None:
        in_specs.append(pl.BlockSpec((1, gw), lambda b, h, i: (0, h)))
        args.append(sink_row)
    return pl.pallas_call(
        functools.partial(_full_attn_kernel, sink=sink_row is not None),
        out_shape=jax.ShapeDtypeStruct((n, Q_HEADS * hd), BF16),
        grid=(nbatch, KV_HEADS, nq),
        in_specs=in_specs,
        out_specs=pl.BlockSpec((tq, gw), lambda b, h, i: (b * nq + i, h)),
        compiler_params=_params("parallel", "parallel", "arbitrary"),
        name="full_attn",
    )(*args)


WIN_BLOCKS_PER_STEP = 8


def _win_attn_kernel(q_ref, kx_ref, vx_ref, kc_ref, vc_ref, sink_ref, o_ref):
    hd = HEAD_DIM
    step = pl.program_id(2)
    t_len = kx_ref.shape[0]
    nsub = q_ref.shape[0] // BLOCK
    span = 3 * BLOCK
    kc = kc_ref[...]
    vc = vc_ref[...]
    l_len = kc.shape[0]
    group = Q_HEADS // KV_HEADS

    def block_keys(j):
        nblk = step * nsub + j
        start = pl.multiple_of(jnp.clip((nblk - 1) * BLOCK, 0, t_len - span), BLOCK)
        k_all = jnp.concatenate([kc, kx_ref[pl.ds(start, span), :]], axis=0)
        v_all = jnp.concatenate([vc, vx_ref[pl.ds(start, span), :]], axis=0)
        col = lax.broadcasted_iota(jnp.int32, (BLOCK, l_len + span), 1)
        qpos = nblk * BLOCK + lax.broadcasted_iota(jnp.int32, (BLOCK, l_len + span), 0)
        allowed = (col < l_len) | (jnp.abs(qpos - (start + col - l_len)) <= WIN)
        return k_all, v_all, allowed

    def chain(j, g, keys):
        k_all, v_all, allowed = keys
        rs = slice(j * BLOCK, (j + 1) * BLOCK)
        sl = slice(g * hd, (g + 1) * hd)
        s = _dot(q_ref[rs, sl], k_all, _NT)
        yield
        s = jnp.where(allowed, s, NEG_INF)
        sk = sink_ref[:, g * hd:g * hd + 1]
        m = jnp.maximum(jnp.max(s, axis=-1, keepdims=True), sk)
        p = jnp.exp(s - m)
        den = jnp.sum(p, axis=-1, keepdims=True) + jnp.exp(sk - m)
        yield
        o_ref[rs, sl] = (_dot(p.astype(BF16), v_all) / den).astype(o_ref.dtype)

    chains = []
    for j in range(nsub):
        keys = block_keys(j)
        chains += [chain(j, g, keys) for g in range(group)]
    _run_chains(chains, 4)


def _win_attn_call(q, kx, vx, kc, vc, sink_row, nbatch):
    n = q.shape[0]
    hd = HEAD_DIM
    t_len = n // nbatch
    l_len = kc.shape[0] // nbatch
    tq = _row_tile(t_len, WIN_BLOCKS_PER_STEP * BLOCK)
    nq = t_len // tq
    gw = (Q_HEADS // KV_HEADS) * hd
    return pl.pallas_call(
        _win_attn_kernel,
        out_shape=jax.ShapeDtypeStruct((n, Q_HEADS * hd), BF16),
        grid=(nbatch, KV_HEADS, nq),
        in_specs=[pl.BlockSpec((tq, gw), lambda b, h, i: (b * nq + i, h)),
                  pl.BlockSpec((t_len, hd), lambda b, h, i: (b, h)),
                  pl.BlockSpec((t_len, hd), lambda b, h, i: (b, h)),
                  pl.BlockSpec((l_len, hd), lambda b, h, i: (b, h)),
                  pl.BlockSpec((l_len, hd), lambda b, h, i: (b, h)),
                  pl.BlockSpec((1, gw), lambda b, h, i: (0, h))],
        out_specs=pl.BlockSpec((tq, gw), lambda b, h, i: (b * nq + i, h)),
        compiler_params=_params("parallel", "parallel", "arbitrary"),
        name="win_attn",
    )(q, kx, vx, kc, vc, sink_row)


MERGE_COL_SPLIT = 2


def _merge_kernel(x_ref, sh_ref, sc_ref, gt_ref, b0_ref, b1_ref, b2_ref, b3_ref, wg_ref, bg_ref,
                  wup_ref, wout_ref, lg_ref, lb_ref, o_ref, hm_ref, acc_ref):
    kb = pl.program_id(1)
    tn = wg_ref.shape[1]

    @pl.when(kb == 0)
    def _():
        hm_ref[...] = (x_ref[...] * (1.0 + sc_ref[0]) + sh_ref[0]).astype(BF16)

    for i, br in enumerate((b0_ref, b1_ref, b2_ref, b3_ref)):
        for h in range(MERGE_COL_SPLIT):
            @pl.when(kb == i * MERGE_COL_SPLIT + h)
            def _(i=i, br=br, h=h):
                cols = slice(h * tn, (h + 1) * tn)
                gate = jax.nn.sigmoid(_dot(hm_ref[...], wg_ref[...]) + bg_ref[...])
                y = gate * _dot(br[...], wup_ref[0])
                if i == 0:
                    acc_ref[:, cols] = y
                else:
                    acc_ref[:, cols] += y

    @pl.when(kb == pl.num_programs(1) - 1)
    def _():
        out = _dot(acc_ref[...].astype(BF16), wout_ref[...])
        y = DEEPNORM_ALPHA * x_ref[...] + gt_ref[0] * out
        o_ref[...] = _layer_norm(y, lg_ref[...], lb_ref[...])


def _merge_call(x, mod_rows, row_of_tile_fn, branches, w_gate, b_gate, wup, wout, lg, lb,
                tm_want=512):
    n, d = x.shape
    c = BRANCH_W
    tm = _row_tile(n, tm_want)
    tn = d // MERGE_COL_SPLIT
    rot = functools.partial(row_of_tile_fn, tm)
    br_spec = pl.BlockSpec((tm, c), lambda i, j: (i, 0))
    return pl.pallas_call(
        _merge_kernel,
        out_shape=jax.ShapeDtypeStruct((n, d), F32),
        grid=(n // tm, N_BRANCH * MERGE_COL_SPLIT),
        in_specs=[pl.BlockSpec((tm, d), lambda i, j: (i, 0)),
                  _mod_spec(d, rot, 3), _mod_spec(d, rot, 4), _mod_spec(d, rot, 5),
                  br_spec, br_spec, br_spec, br_spec,
                  pl.BlockSpec((d, tn), lambda i, j: (0, j)),
                  pl.BlockSpec((1, tn), lambda i, j: (0, j)),
                  pl.BlockSpec((1, c, tn), lambda i, j: (j // MERGE_COL_SPLIT, 0, j % MERGE_COL_SPLIT)),
                  pl.BlockSpec((d, d), lambda i, j: (0, 0), pipeline_mode=pl.Buffered(1)),
                  pl.BlockSpec((1, d), lambda i, j: (0, 0)),
                  pl.BlockSpec((1, d), lambda i, j: (0, 0))],
        out_specs=pl.BlockSpec((tm, d), lambda i, j: (i, 0)),
        scratch_shapes=[pltpu.VMEM((tm, d), BF16), pltpu.VMEM((tm, d), F32)],
        compiler_params=_params("parallel", "arbitrary"),
        name="merge",
    )(x, mod_rows, mod_rows, mod_rows, *branches, w_gate, b_gate.reshape(1, N_BRANCH * d), wup, wout,
      lg.reshape(1, d), lb.reshape(1, d))


def _pad_rows(w, rows):
    return jnp.pad(w, ((0, rows - w.shape[0]), (0, 0)))


def _rwkv_weights(w_rwkv, mu, w0, w2, a0, a2, g2, kk, ka, rk, gn_g, gn_b):
    c = BRANCH_W
    o = 3 * c
    cuts = [o, o + DECAY_LORA, o + 2 * DECAY_LORA, o + 2 * DECAY_LORA + ICLR_LORA,
            o + 2 * DECAY_LORA + 2 * ICLR_LORA]

    def pad_cols(m):
        parts = [m[:, :o]]
        for lo, hi in zip(cuts[:-1], cuts[1:]):
            parts.append(jnp.pad(m[:, lo:hi], ((0, 0), (0, LORA_PAD - (hi - lo)))))
        parts.append(m[:, cuts[-1]:])
        return jnp.concatenate(parts, axis=1)

    def block_diag(m):
        z = jnp.zeros((LORA_PAD, c), m.dtype)
        top = jnp.concatenate([_pad_rows(m[0], LORA_PAD), z], axis=1)
        bot = jnp.concatenate([z, _pad_rows(m[1], LORA_PAD)], axis=1)
        return jnp.concatenate([top, bot], axis=0)

    head = jnp.arange(c) // RWKV_HEAD
    return {
        "w": pad_cols(w_rwkv).astype(BF16),
        "mu": pad_cols(mu.reshape(1, -1)),
        "w0": w0.reshape(1, 2 * c), "w2": block_diag(w2).astype(BF16),
        "a0": a0.reshape(1, 2 * c), "a2": block_diag(a2).astype(BF16),
        "g2": g2.astype(BF16),
        "kk": kk.reshape(1, c), "ka": ka.reshape(1, c), "rk": rk.reshape(1, c),
        "gn_g": gn_g.reshape(1, c), "gn_b": gn_b.reshape(1, c),
        "bd": (head[:, None] == head[None, :]).astype(BF16),
    }


def _rope_tables(t_len):
    rows = t_len // GRID_W
    row = jnp.repeat(jnp.arange(rows), GRID_W).astype(F32)
    col = (jnp.arange(t_len) % GRID_W).astype(F32)
    n_freq = HEAD_DIM // 4
    inv = ROPE_THETA ** (-jnp.arange(n_freq, dtype=F32) / n_freq)
    ar = row[:, None] * inv
    ac = col[:, None] * inv
    cosf = jnp.concatenate([jnp.cos(ar), jnp.cos(ar), jnp.cos(ac), jnp.cos(ac)], axis=1)
    sins = jnp.concatenate([-jnp.sin(ar), jnp.sin(ar), -jnp.sin(ac), jnp.sin(ac)], axis=1)
    return cosf, sins


def kernel(x, c, ctx, c_ctx, w_mod, b_mod, ln_g, ln_b, ffn1_wi, ffn1_wo, ffn2_wi, ffn2_wo, w_in, b_gate, pool_w, pool_scale, rwkv_mu, rwkv_w0, rwkv_w2, rwkv_a0, rwkv_a2, rwkv_g2, rwkv_kk, rwkv_ka, rwkv_rk, rwkv_gn_g, rwkv_gn_b, c_sink, d_qnorm, d_knorm, w_up, w_out):
    nb, t_len, d = x.shape
    l_len = ctx.shape[1]
    depth = w_mod.shape[0]
    cw = BRANCH_W
    assert depth == DEPTH and t_len % GRID_W == 0 and t_len >= 3 * BLOCK

    mod_pad = 16
    c_all = jnp.zeros((mod_pad, d), F32).at[:nb].set(c).at[nb].set(c_ctx)
    cosf, sins = _rope_tables(t_len)
    s0 = jnp.zeros((2, nb, RWKV_HEAD, BRANCH_W), F32)

    def row_x(tm, i):
        return (i * tm) // t_len

    def row_c(tm, i):
        return nb

    xs = x.reshape(nb * t_len, d)
    xc = ctx.reshape(nb * l_len, d)
    o1 = cw
    o2 = o1 + 3 * cw + 2 * DECAY_LORA + 2 * ICLR_LORA + GATE_LORA
    o3 = o2 + ATT_W
    o4 = o3 + ATT_W

    for l in range(depth):
        last = l == depth - 1
        mod_rows = _mod_call(c_all, w_mod[l], b_mod[l]).reshape(mod_pad * N_MOD, 1, d)
        wi1, wo1 = ffn1_wi[l].astype(BF16), ffn1_wo[l].astype(BF16)
        wi2, wo2 = ffn2_wi[l].astype(BF16), ffn2_wo[l].astype(BF16)
        w_pool = w_in[l][:, :o1].astype(BF16)
        pw = _rwkv_weights(w_in[l][:, o1:o2], rwkv_mu[l], rwkv_w0[l], rwkv_w2[l], rwkv_a0[l],
                           rwkv_a2[l], rwkv_g2[l], rwkv_kk[l], rwkv_ka[l], rwkv_rk[l],
                           rwkv_gn_g[l], rwkv_gn_b[l])
        w_win = w_in[l][:, o2:o3].astype(BF16)
        w_dense = w_in[l][:, o3:o4].astype(BF16)
        w_gate = w_in[l][:, o4:].astype(BF16)
        pool_wb = pool_w[l].astype(BF16)
        wup = w_up[l].astype(BF16)
        wout = w_out[l].astype(BF16)
        sink_row = jnp.repeat(c_sink[l], HEAD_DIM).reshape(1, Q_HEADS * HEAD_DIM)

        xs = _ffn_call(xs, mod_rows, row_x, 0, wi1, wo1, ln_g[l, 0], ln_b[l, 0])
        xc = _ffn_call(xc, mod_rows, row_c, 0, wi1, wo1, ln_g[l, 0], ln_b[l, 0])

        proj_x = lambda w: _proj_call(xs, mod_rows, row_x, 3, w)
        proj_c = lambda w: _proj_call(xc, mod_rows, row_c, 3, w)

        prep_c = _rwkv_prep_call(proj_c(pw["w"]), l_len, pw)
        prep_x = _rwkv_prep_call(proj_x(pw["w"]), t_len, pw)
        oc_f, oc_b, s_ctx = _scan_call(prep_c, s0, nb, l_len)
        ox_f, ox_b, _ = _scan_call(prep_x, s_ctx, nb, t_len)
        yb_x = _rwkv_out_call(ox_f, ox_b, prep_x, pw)

        qw_c, kw_c, vw_c, qd_c, kd_c, vd_c = _attn_prep_call(
            proj_c(w_win), proj_c(w_dense), l_len, cosf, sins, d_qnorm[l], d_knorm[l], rope=False)
        qw_x, kw_x, vw_x, qd_x, kd_x, vd_x = _attn_prep_call(
            proj_x(w_win), proj_x(w_dense), t_len, cosf, sins, d_qnorm[l], d_knorm[l], rope=True)
        yc_x = _win_attn_call(qw_x, kw_x, vw_x, kw_c, vw_c, sink_row, nb)
        kv_w = KV_HEADS * HEAD_DIM
        kd_all = jnp.concatenate([kd_c.reshape(nb, l_len, kv_w), kd_x.reshape(nb, t_len, kv_w)],
                                 axis=1).reshape(nb * (l_len + t_len), kv_w)
        vd_all = jnp.concatenate([vd_c.reshape(nb, l_len, kv_w), vd_x.reshape(nb, t_len, kv_w)],
                                 axis=1).reshape(nb * (l_len + t_len), kv_w)
        yd_x = _full_attn_call(qd_x, kd_all, vd_all, nb)

        ya_x = _pool_call(proj_x(w_pool), nb, pool_wb, pool_scale[l])

        xs = _merge_call(xs, mod_rows, row_x, (ya_x, yb_x, yc_x, yd_x), w_gate, b_gate[l], wup, wout,
                         ln_g[l, 1], ln_b[l, 1])
        xs = _ffn_call(xs, mod_rows, row_x, 6, wi2, wo2, ln_g[l, 2], ln_b[l, 2])

        if not last:
            ya_c = _pool_call(proj_c(w_pool), nb, pool_wb, pool_scale[l])
            yb_c = _rwkv_out_call(oc_f, oc_b, prep_c, pw)
            yc_c = _full_attn_call(qw_c, kw_c, vw_c, nb, sink_row)
            yd_c = _full_attn_call(qd_c, kd_c, vd_c, nb)
            xc = _merge_call(xc, mod_rows, row_c, (ya_c, yb_c, yc_c, yd_c), w_gate, b_gate[l], wup, wout,
                             ln_g[l, 1], ln_b[l, 1])
            xc = _ffn_call(xc, mod_rows, row_c, 6, wi2, wo2, ln_g[l, 2], ln_b[l, 2])

    return xs.reshape(nb, t_len, d)
```

```python
import functools
import math

import jax
import jax.numpy as jnp
from jax import lax
from jax.experimental import pallas as pl
from jax.experimental.pallas import tpu as pltpu

F32 = jnp.float32
BF16 = jnp.bfloat16

GRID_W = 64
HEAD_DIM = 128
N_BRANCH = 4
BRANCH_W = 512
POOL_WINDOWS = (2, 4, 8, 16)
POOL_GROUP = BRANCH_W // len(POOL_WINDOWS)
RWKV_HEAD = 64
RWKV_HEADS = BRANCH_W // RWKV_HEAD
DECAY_LORA = 96
ICLR_LORA = 96
GATE_LORA = 256
LORA_PAD = 128
RWKV_W = 3 * BRANCH_W + 4 * LORA_PAD + GATE_LORA
WIN = 128
BLOCK = 128
Q_HEADS = 4
KV_HEADS = 2
ATT_W = (Q_HEADS + 2 * KV_HEADS) * HEAD_DIM
N_MOD = 9
DEPTH = 2
ROPE_THETA = 10000.0
DEEPNORM_ALPHA = (2 * DEPTH) ** 0.25
LN_EPS = 1e-6
RMS_EPS = 1e-6
GN_EPS = 64e-5
NEG_INF = -1e30
DECAY_SCALE = math.exp(-0.5)

V7X_VMEM_BYTES = 64 * 1024 * 1024
VMEM_LIMIT = V7X_VMEM_BYTES - 8 * 1024 * 1024
SCAN_CHUNK = 64

_NT = (((1,), (1,)), ((), ()))
_TN = (((0,), (0,)), ((), ()))


def _params(*sem):
    return pltpu.CompilerParams(dimension_semantics=sem, vmem_limit_bytes=VMEM_LIMIT)


def _dot(a, b, dims=None):
    if dims is None:
        return jnp.dot(a, b, preferred_element_type=F32)
    return lax.dot_general(a, b, dims, preferred_element_type=F32)


def _dot_exact_lhs(a_bf16, b):
    b0 = b.astype(BF16)
    r1 = b - b0.astype(F32)
    b1 = r1.astype(BF16)
    b2 = (r1 - b1.astype(F32)).astype(BF16)
    return _dot(a_bf16, b0) + (_dot(a_bf16, b1) + _dot(a_bf16, b2))


def _dot_exact_rhs(a, b_bf16):
    a0 = a.astype(BF16)
    r1 = a - a0.astype(F32)
    a1 = r1.astype(BF16)
    a2 = (r1 - a1.astype(F32)).astype(BF16)
    return _dot(a0, b_bf16) + (_dot(a1, b_bf16) + _dot(a2, b_bf16))


def _layer_norm(y, g, b):
    mu = jnp.mean(y, axis=-1, keepdims=True)
    yc = y - mu
    var = jnp.mean(yc * yc, axis=-1, keepdims=True)
    return yc * lax.rsqrt(var + LN_EPS) * g + b


def _run_chains(gens, width=None):
    started, pending = [], list(gens)
    width = width or len(pending)
    while started or pending:
        started += pending[:width]
        pending = pending[width:]
        alive = []
        for gen in started:
            try:
                next(gen)
                alive.append(gen)
            except StopIteration:
                pass
        started = alive


def _row_tile(n, want):
    t = min(want, n)
    assert n % t == 0, (n, t)
    return t


def _mod_kernel(c_ref, w_ref, b_ref, o_ref):
    c = c_ref[...]
    s = c * jax.nn.sigmoid(c)
    o_ref[...] = _dot(s.astype(BF16), w_ref[...].astype(BF16)) + b_ref[...]


def _mod_call(c_all, w, b):
    rows, d = c_all.shape
    nout = w.shape[1]
    tn = _row_tile(nout, min(d, 1024))
    return pl.pallas_call(
        _mod_kernel,
        out_shape=jax.ShapeDtypeStruct((rows, nout), F32),
        grid=(nout // tn,),
        in_specs=[pl.BlockSpec((rows, d), lambda j: (0, 0)),
                  pl.BlockSpec((d, tn), lambda j: (0, j)),
                  pl.BlockSpec((1, tn), lambda j: (0, j))],
        out_specs=pl.BlockSpec((rows, tn), lambda j: (0, j)),
        compiler_params=_params("arbitrary"),
        name="mod",
    )(c_all, w, b.reshape(1, nout))


class _ModIndex:
    def __init__(self, base, rows_per_mod):
        self.base = base
        self.rows_per_mod = rows_per_mod

    def tile(self, n, want):
        return _row_tile(min(n, self.rows_per_mod), want)

    def row(self, tm, i):
        return self.base + (i * tm) // self.rows_per_mod


def _mod_spec(d, row_of_tile, k):
    return pl.BlockSpec((1, 1, d), lambda i, j: (row_of_tile(i) * N_MOD + k, 0, 0))


def _ffn_kernel(x_ref, sh_ref, sc_ref, gt_ref, wg_ref, wu_ref, wo_ref, lg_ref, lb_ref,
                o_ref, xm_ref, *, nj):
    j = pl.program_id(1)
    tm = x_ref.shape[0]

    def chain(rs, first, final):
        if first:
            xm = (x_ref[rs, :] * (1.0 + sc_ref[0]) + sh_ref[0]).astype(BF16)
            xm_ref[rs, :] = xm
        else:
            xm = xm_ref[rs, :]
        hg = _dot(xm, wg_ref[...])
        hu = _dot(xm, wu_ref[...])
        yield
        y = _dot((hg * jax.nn.sigmoid(hg) * hu).astype(BF16), wo_ref[...])
        yield
        if final:
            if not first:
                y = o_ref[rs, :] + y
            z = DEEPNORM_ALPHA * x_ref[rs, :] + gt_ref[0] * (0.5 * y)
            o_ref[rs, :] = _layer_norm(z, lg_ref[...], lb_ref[...])
        elif first:
            o_ref[rs, :] = y
        else:
            o_ref[rs, :] += y

    def body(first, final):
        split = (first or final) and tm % 16 == 0
        rows = tm // 2 if split else tm
        _run_chains([chain(slice(r0, r0 + rows), first, final) for r0 in range(0, tm, rows)], 1)

    if nj == 1:
        body(True, True)
    else:
        pl.when(j == 0)(functools.partial(body, True, False))
        pl.when(j == nj - 1)(functools.partial(body, False, True))
        if nj > 2:
            pl.when((j > 0) & (j < nj - 1))(functools.partial(body, False, False))


def _ffn_call(x, mod_rows, mods, k0, wi, wo, lg, lb, tm_want=1024, tf_want=512):
    n, d = x.shape
    dff = wo.shape[0]
    tm = mods.tile(n, tm_want)
    tf = _row_tile(dff, tf_want)
    nj = dff // tf
    rot = functools.partial(mods.row, tm)
    return pl.pallas_call(
        functools.partial(_ffn_kernel, nj=nj),
        out_shape=jax.ShapeDtypeStruct((n, d), F32),
        grid=(n // tm, nj),
        in_specs=[pl.BlockSpec((tm, d), lambda i, j: (i, 0)),
                  _mod_spec(d, rot, k0), _mod_spec(d, rot, k0 + 1), _mod_spec(d, rot, k0 + 2),
                  pl.BlockSpec((d, tf), lambda i, j: (0, j)),
                  pl.BlockSpec((d, tf), lambda i, j: (0, j + nj)),
                  pl.BlockSpec((tf, d), lambda i, j: (j, 0)),
                  pl.BlockSpec((1, d), lambda i, j: (0, 0)),
                  pl.BlockSpec((1, d), lambda i, j: (0, 0))],
        out_specs=pl.BlockSpec((tm, d), lambda i, j: (i, 0)),
        scratch_shapes=[pltpu.VMEM((tm, d), BF16)],
        compiler_params=_params("parallel", "arbitrary"),
        name="ffn",
    )(x, mod_rows, mod_rows, mod_rows, wi, wi, wo, lg.reshape(1, d), lb.reshape(1, d))


def _proj_kernel(x_ref, sh_ref, sc_ref, w_ref, o_ref, xm_ref):
    @pl.when(pl.program_id(1) == 0)
    def _():
        xm_ref[...] = (x_ref[...] * (1.0 + sc_ref[0]) + sh_ref[0]).astype(BF16)

    o_ref[...] = _dot(xm_ref[...], w_ref[...])


def _proj_call(x, mod_rows, mods, k0, w, tm_want=512, tn_want=1024):
    n, d = x.shape
    nout = w.shape[1]
    tm = mods.tile(n, tm_want)
    tn = nout if nout % tn_want else tn_want
    rot = functools.partial(mods.row, tm)
    return pl.pallas_call(
        _proj_kernel,
        out_shape=jax.ShapeDtypeStruct((n, nout), F32),
        grid=(n // tm, nout // tn),
        in_specs=[pl.BlockSpec((tm, d), lambda i, j: (i, 0)),
                  _mod_spec(d, rot, k0), _mod_spec(d, rot, k0 + 1),
                  pl.BlockSpec((d, tn), lambda i, j: (0, j))],
        out_specs=pl.BlockSpec((tm, tn), lambda i, j: (i, j)),
        scratch_shapes=[pltpu.VMEM((tm, d), BF16)],
        compiler_params=_params("parallel", "arbitrary"),
        name="proj",
    )(x, mod_rows, mod_rows, w)


def _pool_kernel(z_ref, w_ref, s_ref, o_ref):
    g = pl.program_id(1)
    u = z_ref[...]
    t_len = u.shape[0]
    t = lax.broadcasted_iota(jnp.int32, u.shape, 0)
    for gi, win in enumerate(POOL_WINDOWS):
        @pl.when(g == gi)
        def _(win=win):
            half = win // 2
            acc = u
            for dlt in range(-half, half):
                if dlt == 0:
                    continue
                shifted = pltpu.roll(u, (-dlt) % t_len, 0)
                ok = (t + dlt >= 0) & (t + dlt < t_len)
                acc = acc + jnp.where(ok, shifted, 0.0)
            cnt = (jnp.minimum(t + half, t_len) - jnp.maximum(t - half, 0)).astype(F32)
            pooled = acc / cnt - u
            y = _dot(pooled.astype(BF16), w_ref[0]) * s_ref[...]
            o_ref[...] = y.astype(o_ref.dtype)


def _pool_call(z, nseq, w_bf16, scale):
    n, c = z.shape
    t_len = n // nseq
    ng = len(POOL_WINDOWS)
    return pl.pallas_call(
        _pool_kernel,
        out_shape=jax.ShapeDtypeStruct((n, c), BF16),
        grid=(nseq, ng),
        in_specs=[pl.BlockSpec((t_len, POOL_GROUP), lambda b, g: (b, g)),
                  pl.BlockSpec((1, POOL_GROUP, POOL_GROUP), lambda b, g: (g, 0, 0)),
                  pl.BlockSpec((1, POOL_GROUP), lambda b, g: (0, g))],
        out_specs=pl.BlockSpec((t_len, POOL_GROUP), lambda b, g: (b, g)),
        compiler_params=_params("parallel", "arbitrary"),
        name="pool",
    )(z, w_bf16, scale.reshape(1, c))


def _rwkv_prep_kernel(z_ref, zp_ref, zn_ref, mu_ref, w0_ref, w2_ref, a0_ref, a2_ref, g2_ref,
                      kkp_ref, ka_ref, bd_ref,
                      r_ref, v_ref, kk_ref, g_ref, k_ref, b_ref, lw_ref, *, tiles_per_seq):
    i = pl.program_id(0)
    cur = z_ref[...]
    tm = cur.shape[0]
    c = BRANCH_W
    row = lax.broadcasted_iota(jnp.int32, cur.shape, 0)
    pos = i % tiles_per_seq
    prev_edge = jnp.where(pos == 0, 0.0, zp_ref[7:8, :])
    next_edge = jnp.where(pos == tiles_per_seq - 1, 0.0, zn_ref[0:1, :])
    prev = jnp.where(row == 0, prev_edge, pltpu.roll(cur, 1, 0))
    nxt = jnp.where(row == tm - 1, next_edge, pltpu.roll(cur, tm - 1, 0))
    zs = cur + (0.5 * (prev + nxt) - cur) * mu_ref[...]

    r = zs[:, :c]
    k = zs[:, c:2 * c]
    v = zs[:, 2 * c:3 * c]
    o = 3 * c
    wd = zs[:, o:o + 2 * LORA_PAD]
    ad = zs[:, o + 2 * LORA_PAD:o + 4 * LORA_PAD]
    gd = zs[:, o + 4 * LORA_PAD:]
    z_w = w0_ref[...] + _dot(jnp.tanh(wd).astype(BF16), w2_ref[...])
    lw = -DECAY_SCALE * jax.nn.sigmoid(z_w)
    a = jax.nn.sigmoid(a0_ref[...] + _dot(ad.astype(BF16), a2_ref[...]))
    g = _dot(jax.nn.sigmoid(gd).astype(BF16), g2_ref[...])
    kk = k * kkp_ref[...]
    ssq = _dot_exact_rhs(kk * kk, bd_ref[...])
    kk = kk * lax.rsqrt(ssq + 1e-12)
    r_ref[...] = r
    v_ref[...] = v
    kk_ref[...] = kk
    g_ref[...] = g
    for dr in range(2):
        a_d = a[:, dr * c:(dr + 1) * c]
        k_ref[dr] = k * (1.0 + (a_d - 1.0) * ka_ref[...])
        b_ref[dr] = kk * a_d
        lw_ref[dr] = lw[:, dr * c:(dr + 1) * c]


def _rwkv_prep_call(z, t_len, pw, tm_want=256):
    n, w = z.shape
    c = BRANCH_W
    tm = _row_tile(t_len, tm_want)
    tps = t_len // tm
    nblk8 = n // 8
    full = lambda shape: pl.BlockSpec(shape, lambda i: (0,) * len(shape))
    tile = pl.BlockSpec((tm, c), lambda i: (i, 0))
    tile2 = pl.BlockSpec((2, tm, c), lambda i: (0, i, 0))
    one = jax.ShapeDtypeStruct((n, c), F32)
    two = jax.ShapeDtypeStruct((2, n, c), F32)
    return pl.pallas_call(
        functools.partial(_rwkv_prep_kernel, tiles_per_seq=tps),
        out_shape=(one, one, one, one, two, two, two),
        grid=(n // tm,),
        in_specs=[pl.BlockSpec((tm, w), lambda i: (i, 0)),
                  pl.BlockSpec((8, w), lambda i: (jnp.maximum(i * (tm // 8) - 1, 0), 0)),
                  pl.BlockSpec((8, w), lambda i: (jnp.minimum((i + 1) * (tm // 8), nblk8 - 1), 0)),
                  full((1, w)), full((1, 2 * c)), full((2 * LORA_PAD, 2 * c)),
                  full((1, 2 * c)), full((2 * LORA_PAD, 2 * c)), full((GATE_LORA, c)),
                  full((1, c)), full((1, c)), full((c, c))],
        out_specs=(tile, tile, tile, tile, tile2, tile2, tile2),
        compiler_params=_params("parallel"),
        name="rwkv_prep",
    )(z, z, z, pw["mu"], pw["w0"], pw["w2"], pw["a0"], pw["a2"], pw["g2"],
      pw["kk"], pw["ka"], pw["bd"])


def _block_diag4(x_bf16, head_mask):
    return jnp.concatenate([x_bf16] * 4, axis=0) * head_mask


def _dot_bd(a, x, head_mask):
    return _dot(a.astype(BF16), _block_diag4(x.astype(BF16), head_mask))


def _scan_masks(rev, cl):
    hw = RWKV_HEAD
    gw = 4 * hw
    row = lax.broadcasted_iota(jnp.int32, (cl, gw), 0)
    lane = lax.broadcasted_iota(jnp.int32, (cl, gw), 1)
    pos_t = (cl - 1 - row) if rev else row
    pos_s = lane % hw
    pos_s = (cl - 1 - pos_s) if rev else pos_s
    nlev = cl.bit_length() - 1
    brow = lax.broadcasted_iota(jnp.int32, (gw, gw), 0) // hw
    blane = lax.broadcasted_iota(jnp.int32, (gw, gw), 1) // hw
    ti = lax.broadcasted_iota(jnp.int32, (cl, cl), 0)
    si = lax.broadcasted_iota(jnp.int32, (cl, cl), 1)
    return {
        "lane_head": lane // hw,
        "strict": pos_s < pos_t,
        "incl": pos_s <= pos_t,
        "eye": jnp.where(pos_s == pos_t, 1.0, 0.0),
        "levels": [((pos_t >> (lev + 1)) == (pos_s >> (lev + 1)))
                   & (((pos_t >> lev) & 1) == 1) & (((pos_s >> lev) & 1) == 0)
                   for lev in range(nlev)],
        "head": jnp.where(brow == blane, 1.0, 0.0).astype(BF16),
        "tri": jnp.where((si >= ti) if rev else (si <= ti), 1.0, 0.0).astype(BF16),
    }


def _scan_group(mk, ab, qb, kb, bb, kh, bh, vb, g_tot, s_ref, sl, out):
    cl = ab.shape[0]
    hw = RWKV_HEAD
    hm = mk["head"]
    aq = jnp.concatenate([ab, qb], axis=0)
    p_b = _dot(aq, _block_diag4(bb, hm), _NT)
    p_k = _dot(aq, _block_diag4(kb, hm), _NT)
    n_ab = jnp.where(mk["strict"], p_b[:cl], 0.0)
    a_ak = jnp.where(mk["strict"], p_k[:cl], 0.0).astype(BF16)
    a_qb = jnp.where(mk["incl"], p_b[cl:], 0.0).astype(BF16)
    a_qk = jnp.where(mk["incl"], p_k[cl:], 0.0).astype(BF16)
    yield
    t_inv = mk["eye"] - jnp.where(mk["levels"][0], n_ab, 0.0)
    for lev in range(1, len(mk["levels"])):
        x = _dot_bd(jnp.where(mk["levels"][lev], n_ab, 0.0), t_inv, hm)
        yield
        t_inv = t_inv - _dot_bd(t_inv, x, hm)
        yield
    s_g = s_ref[:, sl]
    p_s = _dot(aq, _block_diag4(s_g.astype(BF16), hm), _NT)
    p_v = _dot(jnp.concatenate([a_ak, a_qk], axis=0), _block_diag4(vb, hm))
    yield
    ub = (-_dot_bd(t_inv, p_s[:cl] + p_v[:cl], hm)).astype(BF16)
    yield
    out.append(p_s[cl:] + p_v[cl:] + _dot(a_qb, _block_diag4(ub, hm)))
    full = _dot(jnp.concatenate([ub, vb], axis=0), jnp.concatenate([bh, kh], axis=0), _TN)
    upd = jnp.where(mk["lane_head"] == 0, full[:hw], 0.0)
    for h in range(1, 4):
        upd = upd + jnp.where(mk["lane_head"] == h, full[h * hw:(h + 1) * hw], 0.0)
    s_ref[:, sl] = s_g * g_tot + upd


def _scan_direction(mk, r, v, kk, k, b, lw, s_ref, out):
    cl, c = lw.shape
    gw = 4 * RWKV_HEAD
    assert cl == RWKV_HEAD
    cum = _dot_exact_lhs(mk["tri"], lw)
    tot = jnp.sum(lw, axis=0, keepdims=True)
    g_inv = jnp.exp(-cum)
    g_end = jnp.exp(tot - cum)
    g_tot = jnp.exp(tot)
    qb = (r * jnp.exp(cum)).astype(BF16)
    ab = (kk * jnp.exp(cum - lw)).astype(BF16)
    kb = (k * g_inv).astype(BF16)
    bb = (b * g_inv).astype(BF16)
    kh = (k * g_end).astype(BF16)
    bh = (b * g_end).astype(BF16)
    vb = v.astype(BF16)
    gens = []
    for g in range(c // gw):
        sl = slice(g * gw, (g + 1) * gw)
        gens.append(_scan_group(mk, ab[:, sl], qb[:, sl], kb[:, sl], bb[:, sl], kh[:, sl], bh[:, sl],
                                vb[:, sl], g_tot[:, sl], s_ref, sl, out))
    return gens


SCAN_ROWS_PER_STEP = 4


def _scan_kernel(rf_ref, rb_ref, vf_ref, vb_ref, kkf_ref, kkb_ref, kf_ref, kb_ref, bf_ref, bb_ref,
                 lwf_ref, lwb_ref, s0_ref, of_ref, ob_ref, sfin_ref, s_ref):
    n = pl.program_id(1)

    @pl.when(n == 0)
    def _():
        s_ref[...] = s0_ref[...]

    nrow, cl, _ = rf_ref.shape
    mk_f = _scan_masks(False, cl)
    mk_b = _scan_masks(True, cl)
    gens, outs = [], []
    for i in range(nrow):
        out_f, out_b = [], []
        outs.append((out_f, out_b))
        gens += _scan_direction(mk_f, rf_ref[i], vf_ref[i], kkf_ref[i], kf_ref[i], bf_ref[i],
                                lwf_ref[i], s_ref.at[0, i], out_f)
        gens += _scan_direction(mk_b, rb_ref[i], vb_ref[i], kkb_ref[i], kb_ref[i], bb_ref[i],
                                lwb_ref[i], s_ref.at[1, i], out_b)
    _run_chains(gens)
    for i, (out_f, out_b) in enumerate(outs):
        of_ref[i] = jnp.concatenate(out_f, axis=1)
        ob_ref[i] = jnp.concatenate(out_b, axis=1)

    @pl.when(n == pl.num_programs(1) - 1)
    def _():
        sfin_ref[...] = s_ref[...]


def _scan_call(prep, s0, nbatch, t_len):
    r, v, kk, _, k2, b2, lw2 = prep
    n, c = r.shape
    cl = SCAN_CHUNK
    nc = t_len // cl
    nrow = math.gcd(nbatch, SCAN_ROWS_PER_STEP)
    seq = lambda a: a.reshape(a.shape[:-2] + (nbatch, t_len, c))
    fwd = pl.BlockSpec((nrow, cl, c), lambda bi, i: (bi, i, 0))
    bwd = pl.BlockSpec((nrow, cl, c), lambda bi, i: (bi, nc - 1 - i, 0))
    fwd2 = pl.BlockSpec((None, nrow, cl, c), lambda bi, i: (0, bi, i, 0))
    bwd2 = pl.BlockSpec((None, nrow, cl, c), lambda bi, i: (1, bi, nc - 1 - i, 0))
    state = pl.BlockSpec((2, nrow, RWKV_HEAD, c), lambda bi, i: (0, bi, 0, 0))
    o_f, o_b, s_fin = pl.pallas_call(
        _scan_kernel,
        out_shape=(jax.ShapeDtypeStruct((nbatch, t_len, c), F32),
                   jax.ShapeDtypeStruct((nbatch, t_len, c), F32),
                   jax.ShapeDtypeStruct(s0.shape, F32)),
        grid=(nbatch // nrow, nc),
        in_specs=[fwd, bwd, fwd, bwd, fwd, bwd, fwd2, bwd2, fwd2, bwd2, fwd2, bwd2, state],
        out_specs=(fwd, bwd, state),
        scratch_shapes=[pltpu.VMEM((2, nrow, RWKV_HEAD, c), F32)],
        compiler_params=_params("parallel", "arbitrary"),
        name="rwkv_scan",
    )(seq(r), seq(r), seq(v), seq(v), seq(kk), seq(kk), seq(k2), seq(k2), seq(b2), seq(b2),
      seq(lw2), seq(lw2), s0)
    return o_f.reshape(n, c), o_b.reshape(n, c), s_fin


def _rwkv_out_kernel(of_ref, ob_ref, r_ref, k_ref, v_ref, g_ref, rk_ref, gg_ref, gb_ref, bd_ref, y_ref):
    bd = bd_ref[...]
    inv_n = 1.0 / RWKV_HEAD
    s = of_ref[...] + ob_ref[...]
    mu = _dot_exact_rhs(s, bd) * inv_n
    sc = s - mu
    var = _dot_exact_rhs(sc * sc, bd) * inv_n
    y = sc * lax.rsqrt(var + GN_EPS) * gg_ref[...] + gb_ref[...]
    r = r_ref[...]
    bonus = _dot_exact_rhs(r * k_ref[0] * rk_ref[...] + r * k_ref[1] * rk_ref[...], bd)
    y_ref[...] = ((y + bonus * v_ref[...]) * g_ref[...]).astype(y_ref.dtype)


def _rwkv_out_call(o_f, o_b, prep, pw, tm_want=512):
    r, v, _, g, k2, _, _ = prep
    n, c = r.shape
    tm = _row_tile(n, tm_want)
    tile = pl.BlockSpec((tm, c), lambda i: (i, 0))
    tile2 = pl.BlockSpec((2, tm, c), lambda i: (0, i, 0))
    vec = pl.BlockSpec((1, c), lambda i: (0, 0))
    return pl.pallas_call(
        _rwkv_out_kernel,
        out_shape=jax.ShapeDtypeStruct((n, c), BF16),
        grid=(n // tm,),
        in_specs=[tile, tile, tile, tile2, tile, tile, vec, vec, vec,
                  pl.BlockSpec((c, c), lambda i: (0, 0))],
        out_specs=tile,
        compiler_params=_params("parallel"),
        name="rwkv_out",
    )(o_f, o_b, r, k2, v, g, pw["rk"], pw["gn_g"], pw["gn_b"], pw["bd"])


def _attn_prep_kernel(zw_ref, zd_ref, cos_ref, sin_ref, qn_ref, kn_ref,
                      qw_ref, kw_ref, vw_ref, qd_ref, kd_ref, vd_ref, *, rope):
    scale = HEAD_DIM ** -0.5
    hd = HEAD_DIM
    if rope:
        cosf = cos_ref[...]
        sins = sin_ref[...]
        lane = lax.broadcasted_iota(jnp.int32, cosf.shape, 1)
        first = (lane % (hd // 2)) < (hd // 4)

    def rot(xh):
        if not rope:
            return xh
        partner = jnp.where(first, pltpu.roll(xh, hd - hd // 4, 1), pltpu.roll(xh, hd // 4, 1))
        return xh * cosf + partner * sins

    def rms(xh, gain):
        return xh * lax.rsqrt(jnp.mean(xh * xh, axis=-1, keepdims=True) + RMS_EPS) * gain

    for h in range(Q_HEADS):
        sl = slice(h * hd, (h + 1) * hd)
        qw_ref[:, sl] = (rot(zw_ref[:, sl]) * scale).astype(BF16)
        qd_ref[:, sl] = (rot(rms(zd_ref[:, sl], qn_ref[...])) * scale).astype(BF16)
    for h in range(KV_HEADS):
        sl = slice(h * hd, (h + 1) * hd)
        ks = slice((Q_HEADS + h) * hd, (Q_HEADS + h + 1) * hd)
        vs = slice((Q_HEADS + KV_HEADS + h) * hd, (Q_HEADS + KV_HEADS + h + 1) * hd)
        kw_ref[:, sl] = rot(zw_ref[:, ks]).astype(BF16)
        kd_ref[:, sl] = rot(rms(zd_ref[:, ks], kn_ref[...])).astype(BF16)
        vw_ref[:, sl] = zw_ref[:, vs].astype(BF16)
        vd_ref[:, sl] = zd_ref[:, vs].astype(BF16)


def _attn_prep_call(zw, zd, t_len, cosf, sins, qn, kn, rope, tm_want=256):
    n = zw.shape[0]
    tm = _row_tile(t_len, tm_want)
    tps = t_len // tm
    hd = HEAD_DIM
    tile = lambda w: pl.BlockSpec((tm, w), lambda i: (i, 0))
    pos = pl.BlockSpec((tm, hd), lambda i: (i % tps, 0))
    vec = pl.BlockSpec((1, hd), lambda i: (0, 0))
    qs = jax.ShapeDtypeStruct((n, Q_HEADS * hd), BF16)
    ks = jax.ShapeDtypeStruct((n, KV_HEADS * hd), BF16)
    return pl.pallas_call(
        functools.partial(_attn_prep_kernel, rope=rope),
        out_shape=(qs, ks, ks, qs, ks, ks),
        grid=(n // tm,),
        in_specs=[tile(ATT_W), tile(ATT_W), pos, pos, vec, vec],
        out_specs=(tile(Q_HEADS * hd), tile(KV_HEADS * hd), tile(KV_HEADS * hd),
                   tile(Q_HEADS * hd), tile(KV_HEADS * hd), tile(KV_HEADS * hd)),
        compiler_params=_params("parallel"),
        name="attn_prep",
    )(zw, zd, cosf, sins, qn.reshape(1, hd), kn.reshape(1, hd))


def _full_attn_kernel(q_ref, k_ref, v_ref, *rest, sink):
    if sink:
        sink_ref, o_ref = rest
    else:
        (o_ref,) = rest
    hd = HEAD_DIM
    k = k_ref[...]
    v = v_ref[...]
    group = Q_HEADS // KV_HEADS
    tq = q_ref.shape[0]
    rows = min(tq, FULL_ATTN_CHAIN_ROWS)

    def chain(g, r0):
        sl = slice(g * hd, (g + 1) * hd)
        rs = slice(r0, r0 + rows)
        s = _dot(q_ref[rs, sl], k, _NT)
        yield
        m = jnp.max(s, axis=-1, keepdims=True)
        if sink:
            sk = sink_ref[:, g * hd:g * hd + 1]
            m = jnp.maximum(m, sk)
        p = jnp.exp(s - m)
        den = jnp.sum(p, axis=-1, keepdims=True)
        if sink:
            den = den + jnp.exp(sk - m)
        yield
        o_ref[rs, sl] = (_dot(p.astype(BF16), v) / den).astype(o_ref.dtype)

    _run_chains([chain(g, r0) for g in range(group) for r0 in range(0, tq, rows)], 1)


FULL_ATTN_CHAIN_ROWS = 128


def _full_attn_call(q, k, v, nbatch, sink_row=None, tq_want=512):
    n = q.shape[0]
    hd = HEAD_DIM
    tq_len = n // nbatch
    tk_len = k.shape[0] // nbatch
    tq = _row_tile(tq_len, tq_want)
    nq = tq_len // tq
    gw = (Q_HEADS // KV_HEADS) * hd
    in_specs = [pl.BlockSpec((tq, gw), lambda b, h, i: (b * nq + i, h)),
                pl.BlockSpec((tk_len, hd), lambda b, h, i: (b, h)),
                pl.BlockSpec((tk_len, hd), lambda b, h, i: (b, h))]
    args = [q, k, v]
    if sink_row is not None:
        in_specs.append(pl.BlockSpec((1, gw), lambda b, h, i: (0, h)))
        args.append(sink_row)
    return pl.pallas_call(
        functools.partial(_full_attn_kernel, sink=sink_row is not None),
        out_shape=jax.ShapeDtypeStruct((n, Q_HEADS * hd), BF16),
        grid=(nbatch, KV_HEADS, nq),
        in_specs=in_specs,
        out_specs=pl.BlockSpec((tq, gw), lambda b, h, i: (b * nq + i, h)),
        compiler_params=_params("parallel", "parallel", "arbitrary"),
        name="full_attn",
    )(*args)


WIN_BLOCKS_PER_STEP = 8


def _win_attn_kernel(q_ref, kx_ref, vx_ref, kc_ref, vc_ref, sink_ref, o_ref):
    hd = HEAD_DIM
    step = pl.program_id(2)
    t_len = kx_ref.shape[0]
    nsub = q_ref.shape[0] // BLOCK
    span = 3 * BLOCK
    kc = kc_ref[...]
    vc = vc_ref[...]
    l_len = kc.shape[0]
    group = Q_HEADS // KV_HEADS

    def block_keys(j):
        nblk = step * nsub + j
        start = pl.multiple_of(jnp.clip((nblk - 1) * BLOCK, 0, t_len - span), BLOCK)
        k_all = jnp.concatenate([kc, kx_ref[pl.ds(start, span), :]], axis=0)
        v_all = jnp.concatenate([vc, vx_ref[pl.ds(start, span), :]], axis=0)
        col = lax.broadcasted_iota(jnp.int32, (BLOCK, l_len + span), 1)
        qpos = nblk * BLOCK + lax.broadcasted_iota(jnp.int32, (BLOCK, l_len + span), 0)
        allowed = (col < l_len) | (jnp.abs(qpos - (start + col - l_len)) <= WIN)
        return k_all, v_all, allowed

    def chain(j, g, keys):
        k_all, v_all, allowed = keys
        rs = slice(j * BLOCK, (j + 1) * BLOCK)
        sl = slice(g * hd, (g + 1) * hd)
        s = _dot(q_ref[rs, sl], k_all, _NT)
        yield
        s = jnp.where(allowed, s, NEG_INF)
        sk = sink_ref[:, g * hd:g * hd + 1]
        m = jnp.maximum(jnp.max(s, axis=-1, keepdims=True), sk)
        p = jnp.exp(s - m)
        den = jnp.sum(p, axis=-1, keepdims=True) + jnp.exp(sk - m)
        yield
        o_ref[rs, sl] = (_dot(p.astype(BF16), v_all) / den).astype(o_ref.dtype)

    chains = []
    for j in range(nsub):
        keys = block_keys(j)
        chains += [chain(j, g, keys) for g in range(group)]
    _run_chains(chains, 4)


def _win_attn_call(q, kx, vx, kc, vc, sink_row, nbatch):
    n = q.shape[0]
    hd = HEAD_DIM
    t_len = n // nbatch
    l_len = kc.shape[0] // nbatch
    tq = _row_tile(t_len, WIN_BLOCKS_PER_STEP * BLOCK)
    nq = t_len // tq
    gw = (Q_HEADS // KV_HEADS) * hd
    return pl.pallas_call(
        _win_attn_kernel,
        out_shape=jax.ShapeDtypeStruct((n, Q_HEADS * hd), BF16),
        grid=(nbatch, KV_HEADS, nq),
        in_specs=[pl.BlockSpec((tq, gw), lambda b, h, i: (b * nq + i, h)),
                  pl.BlockSpec((t_len, hd), lambda b, h, i: (b, h)),
                  pl.BlockSpec((t_len, hd), lambda b, h, i: (b, h)),
                  pl.BlockSpec((l_len, hd), lambda b, h, i: (b, h)),
                  pl.BlockSpec((l_len, hd), lambda b, h, i: (b, h)),
                  pl.BlockSpec((1, gw), lambda b, h, i: (0, h))],
        out_specs=pl.BlockSpec((tq, gw), lambda b, h, i: (b * nq + i, h)),
        compiler_params=_params("parallel", "parallel", "arbitrary"),
        name="win_attn",
    )(q, kx, vx, kc, vc, sink_row)


MERGE_COL_SPLIT = 2


def _merge_kernel(x_ref, sh_ref, sc_ref, gt_ref, b0_ref, b1_ref, b2_ref, b3_ref, wg_ref, bg_ref,
                  wup_ref, wout_ref, lg_ref, lb_ref, o_ref, hm_ref, acc_ref):
    kb = pl.program_id(1)
    tn = wg_ref.shape[1]

    @pl.when(kb == 0)
    def _():
        hm_ref[...] = (x_ref[...] * (1.0 + sc_ref[0]) + sh_ref[0]).astype(BF16)

    for i, br in enumerate((b0_ref, b1_ref, b2_ref, b3_ref)):
        for h in range(MERGE_COL_SPLIT):
            @pl.when(kb == i * MERGE_COL_SPLIT + h)
            def _(i=i, br=br, h=h):
                cols = slice(h * tn, (h + 1) * tn)
                gate = jax.nn.sigmoid(_dot(hm_ref[...], wg_ref[...]) + bg_ref[...])
                y = gate * _dot(br[...], wup_ref[0])
                if i == 0:
                    acc_ref[:, cols] = y
                else:
                    acc_ref[:, cols] += y

    @pl.when(kb == pl.num_programs(1) - 1)
    def _():
        out = _dot(acc_ref[...].astype(BF16), wout_ref[...])
        y = DEEPNORM_ALPHA * x_ref[...] + gt_ref[0] * out
        o_ref[...] = _layer_norm(y, lg_ref[...], lb_ref[...])


def _merge_call(x, mod_rows, mods, branches, w_gate, b_gate, wup, wout, lg, lb,
                tm_want=512):
    n, d = x.shape
    c = BRANCH_W
    tm = mods.tile(n, tm_want)
    tn = d // MERGE_COL_SPLIT
    rot = functools.partial(mods.row, tm)
    br_spec = pl.BlockSpec((tm, c), lambda i, j: (i, 0))
    return pl.pallas_call(
        _merge_kernel,
        out_shape=jax.ShapeDtypeStruct((n, d), F32),
        grid=(n // tm, N_BRANCH * MERGE_COL_SPLIT),
        in_specs=[pl.BlockSpec((tm, d), lambda i, j: (i, 0)),
                  _mod_spec(d, rot, 3), _mod_spec(d, rot, 4), _mod_spec(d, rot, 5),
                  br_spec, br_spec, br_spec, br_spec,
                  pl.BlockSpec((d, tn), lambda i, j: (0, j)),
                  pl.BlockSpec((1, tn), lambda i, j: (0, j)),
                  pl.BlockSpec((1, c, tn), lambda i, j: (j // MERGE_COL_SPLIT, 0, j % MERGE_COL_SPLIT)),
                  pl.BlockSpec((d, d), lambda i, j: (0, 0), pipeline_mode=pl.Buffered(1)),
                  pl.BlockSpec((1, d), lambda i, j: (0, 0)),
                  pl.BlockSpec((1, d), lambda i, j: (0, 0))],
        out_specs=pl.BlockSpec((tm, d), lambda i, j: (i, 0)),
        scratch_shapes=[pltpu.VMEM((tm, d), BF16), pltpu.VMEM((tm, d), F32)],
        compiler_params=_params("parallel", "arbitrary"),
        name="merge",
    )(x, mod_rows, mod_rows, mod_rows, *branches, w_gate, b_gate.reshape(1, N_BRANCH * d), wup, wout,
      lg.reshape(1, d), lb.reshape(1, d))


def _pad_rows(w, rows):
    return jnp.pad(w, ((0, rows - w.shape[0]), (0, 0)))


def _rwkv_weights(w_rwkv, mu, w0, w2, a0, a2, g2, kk, ka, rk, gn_g, gn_b):
    c = BRANCH_W
    o = 3 * c
    cuts = [o, o + DECAY_LORA, o + 2 * DECAY_LORA, o + 2 * DECAY_LORA + ICLR_LORA,
            o + 2 * DECAY_LORA + 2 * ICLR_LORA]

    def pad_cols(m):
        parts = [m[:, :o]]
        for lo, hi in zip(cuts[:-1], cuts[1:]):
            parts.append(jnp.pad(m[:, lo:hi], ((0, 0), (0, LORA_PAD - (hi - lo)))))
        parts.append(m[:, cuts[-1]:])
        return jnp.concatenate(parts, axis=1)

    def block_diag(m):
        z = jnp.zeros((LORA_PAD, c), m.dtype)
        top = jnp.concatenate([_pad_rows(m[0], LORA_PAD), z], axis=1)
        bot = jnp.concatenate([z, _pad_rows(m[1], LORA_PAD)], axis=1)
        return jnp.concatenate([top, bot], axis=0)

    head = jnp.arange(c) // RWKV_HEAD
    return {
        "w": pad_cols(w_rwkv).astype(BF16),
        "mu": pad_cols(mu.reshape(1, -1)),
        "w0": w0.reshape(1, 2 * c), "w2": block_diag(w2).astype(BF16),
        "a0": a0.reshape(1, 2 * c), "a2": block_diag(a2).astype(BF16),
        "g2": g2.astype(BF16),
        "kk": kk.reshape(1, c), "ka": ka.reshape(1, c), "rk": rk.reshape(1, c),
        "gn_g": gn_g.reshape(1, c), "gn_b": gn_b.reshape(1, c),
        "bd": (head[:, None] == head[None, :]).astype(BF16),
    }


def _rope_tables(t_len):
    rows = t_len // GRID_W
    row = jnp.repeat(jnp.arange(rows), GRID_W).astype(F32)
    col = (jnp.arange(t_len) % GRID_W).astype(F32)
    n_freq = HEAD_DIM // 4
    inv = ROPE_THETA ** (-jnp.arange(n_freq, dtype=F32) / n_freq)
    ar = row[:, None] * inv
    ac = col[:, None] * inv
    cosf = jnp.concatenate([jnp.cos(ar), jnp.cos(ar), jnp.cos(ac), jnp.cos(ac)], axis=1)
    sins = jnp.concatenate([-jnp.sin(ar), jnp.sin(ar), -jnp.sin(ac), jnp.sin(ac)], axis=1)
    return cosf, sins


def kernel(x, c, ctx, c_ctx, w_mod, b_mod, ln_g, ln_b, ffn1_wi, ffn1_wo, ffn2_wi, ffn2_wo, w_in, b_gate, pool_w, pool_scale, rwkv_mu, rwkv_w0, rwkv_w2, rwkv_a0, rwkv_a2, rwkv_g2, rwkv_kk, rwkv_ka, rwkv_rk, rwkv_gn_g, rwkv_gn_b, c_sink, d_qnorm, d_knorm, w_up, w_out):
    nb, t_len, d = x.shape
    l_len = ctx.shape[1]
    depth = w_mod.shape[0]
    cw = BRANCH_W
    assert depth == DEPTH and t_len % GRID_W == 0 and t_len >= 3 * BLOCK

    mod_pad = 16
    c_all = jnp.zeros((mod_pad, d), F32).at[:nb].set(c).at[nb].set(c_ctx)
    cosf, sins = _rope_tables(t_len)
    s0 = jnp.zeros((2, nb, RWKV_HEAD, BRANCH_W), F32)

    row_x = _ModIndex(0, t_len)
    row_c = _ModIndex(nb, nb * l_len)

    xs = x.reshape(nb * t_len, d)
    xc = ctx.reshape(nb * l_len, d)
    o1 = cw
    o2 = o1 + 3 * cw + 2 * DECAY_LORA + 2 * ICLR_LORA + GATE_LORA
    o3 = o2 + ATT_W
    o4 = o3 + ATT_W

    for l in range(depth):
        last = l == depth - 1
        mod_rows = _mod_call(c_all, w_mod[l], b_mod[l]).reshape(mod_pad * N_MOD, 1, d)
        wi1, wo1 = ffn1_wi[l].astype(BF16), ffn1_wo[l].astype(BF16)
        wi2, wo2 = ffn2_wi[l].astype(BF16), ffn2_wo[l].astype(BF16)
        w_pool = w_in[l][:, :o1].astype(BF16)
        pw = _rwkv_weights(w_in[l][:, o1:o2], rwkv_mu[l], rwkv_w0[l], rwkv_w2[l], rwkv_a0[l],
                           rwkv_a2[l], rwkv_g2[l], rwkv_kk[l], rwkv_ka[l], rwkv_rk[l],
                           rwkv_gn_g[l], rwkv_gn_b[l])
        w_win = w_in[l][:, o2:o3].astype(BF16)
        w_dense = w_in[l][:, o3:o4].astype(BF16)
        w_gate = w_in[l][:, o4:].astype(BF16)
        pool_wb = pool_w[l].astype(BF16)
        wup = w_up[l].astype(BF16)
        wout = w_out[l].astype(BF16)
        sink_row = jnp.repeat(c_sink[l], HEAD_DIM).reshape(1, Q_HEADS * HEAD_DIM)

        xs = _ffn_call(xs, mod_rows, row_x, 0, wi1, wo1, ln_g[l, 0], ln_b[l, 0])
        xc = _ffn_call(xc, mod_rows, row_c, 0, wi1, wo1, ln_g[l, 0], ln_b[l, 0])

        proj_x = lambda w: _proj_call(xs, mod_rows, row_x, 3, w)
        proj_c = lambda w: _proj_call(xc, mod_rows, row_c, 3, w)

        prep_c = _rwkv_prep_call(proj_c(pw["w"]), l_len, pw)
        prep_x = _rwkv_prep_call(proj_x(pw["w"]), t_len, pw)
        oc_f, oc_b, s_ctx = _scan_call(prep_c, s0, nb, l_len)
        ox_f, ox_b, _ = _scan_call(prep_x, s_ctx, nb, t_len)
        yb_x = _rwkv_out_call(ox_f, ox_b, prep_x, pw)

        qw_c, kw_c, vw_c, qd_c, kd_c, vd_c = _attn_prep_call(
            proj_c(w_win), proj_c(w_dense), l_len, cosf, sins, d_qnorm[l], d_knorm[l], rope=False)
        qw_x, kw_x, vw_x, qd_x, kd_x, vd_x = _attn_prep_call(
            proj_x(w_win), proj_x(w_dense), t_len, cosf, sins, d_qnorm[l], d_knorm[l], rope=True)
        yc_x = _win_attn_call(qw_x, kw_x, vw_x, kw_c, vw_c, sink_row, nb)
        kv_w = KV_HEADS * HEAD_DIM
        kd_all = jnp.concatenate([kd_c.reshape(nb, l_len, kv_w), kd_x.reshape(nb, t_len, kv_w)],
                                 axis=1).reshape(nb * (l_len + t_len), kv_w)
        vd_all = jnp.concatenate([vd_c.reshape(nb, l_len, kv_w), vd_x.reshape(nb, t_len, kv_w)],
                                 axis=1).reshape(nb * (l_len + t_len), kv_w)
        yd_x = _full_attn_call(qd_x, kd_all, vd_all, nb)

        ya_x = _pool_call(proj_x(w_pool), nb, pool_wb, pool_scale[l])

        xs = _merge_call(xs, mod_rows, row_x, (ya_x, yb_x, yc_x, yd_x), w_gate, b_gate[l], wup, wout,
                         ln_g[l, 1], ln_b[l, 1])
        xs = _ffn_call(xs, mod_rows, row_x, 6, wi2, wo2, ln_g[l, 2], ln_b[l, 2])

        if not last:
            ya_c = _pool_call(proj_c(w_pool), nb, pool_wb, pool_scale[l])
            yb_c = _rwkv_out_call(oc_f, oc_b, prep_c, pw)
            yc_c = _full_attn_call(qw_c, kw_c, vw_c, nb, sink_row)
            yd_c = _full_attn_call(qd_c, kd_c, vd_c, nb)
            xc = _merge_call(xc, mod_rows, row_c, (ya_c, yb_c, yc_c, yd_c), w_gate, b_gate[l], wup, wout,
                             ln_g[l, 1], ln_b[l, 1])
            xc = _ffn_call(xc, mod_rows, row_c, 6, wi2, wo2, ln_g[l, 2], ln_b[l, 2])

    return xs.reshape(nb, t_len, d)
```

```python
import functools
import math

import jax
import jax.numpy as jnp
from jax import lax
from jax.experimental import pallas as pl
from jax.experimental.pallas import tpu as pltpu

F32 = jnp.float32
BF16 = jnp.bfloat16

GRID_W = 64
HEAD_DIM = 128
N_BRANCH = 4
BRANCH_W = 512
POOL_WINDOWS = (2, 4, 8, 16)
POOL_GROUP = BRANCH_W // len(POOL_WINDOWS)
RWKV_HEAD = 64
DECAY_LORA = 96
ICLR_LORA = 96
GATE_LORA = 256
LORA_PAD = 128
WIN = 128
BLOCK = 128
Q_HEADS = 4
KV_HEADS = 2
ATT_W = (Q_HEADS + 2 * KV_HEADS) * HEAD_DIM
N_MOD = 9
DEPTH = 2
ROPE_THETA = 10000.0
DEEPNORM_ALPHA = (2 * DEPTH) ** 0.25
LN_EPS = 1e-6
RMS_EPS = 1e-6
GN_EPS = 64e-5
NEG_INF = -1e30
DECAY_SCALE = math.exp(-0.5)

V7X_VMEM_BYTES = 64 * 1024 * 1024
VMEM_LIMIT = V7X_VMEM_BYTES - 8 * 1024 * 1024
SCAN_CHUNK = 64

_NT = (((1,), (1,)), ((), ()))
_TN = (((0,), (0,)), ((), ()))


def _params(*sem):
    return pltpu.CompilerParams(dimension_semantics=sem, vmem_limit_bytes=VMEM_LIMIT)


def _dot(a, b, dims=None):
    if dims is None:
        return jnp.dot(a, b, preferred_element_type=F32)
    return lax.dot_general(a, b, dims, preferred_element_type=F32)


def _dot_exact_lhs(a_bf16, b):
    b0 = b.astype(BF16)
    r1 = b - b0.astype(F32)
    b1 = r1.astype(BF16)
    b2 = (r1 - b1.astype(F32)).astype(BF16)
    return _dot(a_bf16, b0) + (_dot(a_bf16, b1) + _dot(a_bf16, b2))


def _dot_exact_rhs(a, b_bf16):
    a0 = a.astype(BF16)
    r1 = a - a0.astype(F32)
    a1 = r1.astype(BF16)
    a2 = (r1 - a1.astype(F32)).astype(BF16)
    return _dot(a0, b_bf16) + (_dot(a1, b_bf16) + _dot(a2, b_bf16))


def _layer_norm(y, g, b):
    mu = jnp.mean(y, axis=-1, keepdims=True)
    yc = y - mu
    var = jnp.mean(yc * yc, axis=-1, keepdims=True)
    return yc * lax.rsqrt(var + LN_EPS) * g + b


def _run_chains(gens, width=None):
    started, pending = [], list(gens)
    width = width or len(pending)
    while started or pending:
        started += pending[:width]
        pending = pending[width:]
        alive = []
        for gen in started:
            try:
                next(gen)
                alive.append(gen)
            except StopIteration:
                pass
        started = alive


def _row_tile(n, want):
    t = min(want, n)
    assert n % t == 0, (n, t)
    return t


def _mod_kernel(c_ref, w_ref, b_ref, o_ref):
    c = c_ref[...]
    s = c * jax.nn.sigmoid(c)
    o_ref[...] = _dot(s.astype(BF16), w_ref[...].astype(BF16)) + b_ref[...]


def _mod_call(c_all, w, b):
    rows, d = c_all.shape
    nout = w.shape[1]
    tn = _row_tile(nout, min(d, 1024))
    return pl.pallas_call(
        _mod_kernel,
        out_shape=jax.ShapeDtypeStruct((rows, nout), F32),
        grid=(nout // tn,),
        in_specs=[pl.BlockSpec((rows, d), lambda j: (0, 0)),
                  pl.BlockSpec((d, tn), lambda j: (0, j)),
                  pl.BlockSpec((1, tn), lambda j: (0, j))],
        out_specs=pl.BlockSpec((rows, tn), lambda j: (0, j)),
        compiler_params=_params("arbitrary"),
        name="mod",
    )(c_all, w, b.reshape(1, nout))


class _ModIndex:
    def __init__(self, base, rows_per_mod):
        self.base = base
        self.rows_per_mod = rows_per_mod

    def tile(self, n, want):
        return _row_tile(min(n, self.rows_per_mod), want)

    def row(self, tm, i):
        return self.base + (i * tm) // self.rows_per_mod


def _mod_spec(d, row_of_tile, k):
    return pl.BlockSpec((1, 1, d), lambda i, j: (row_of_tile(i) * N_MOD + k, 0, 0))


def _ffn_kernel(x_ref, sh_ref, sc_ref, gt_ref, wg_ref, wu_ref, wo_ref, lg_ref, lb_ref,
                o_ref, xm_ref, *, nj):
    j = pl.program_id(1)
    tm = x_ref.shape[0]

    def chain(rs, first, final):
        if first:
            xm = (x_ref[rs, :] * (1.0 + sc_ref[0]) + sh_ref[0]).astype(BF16)
            xm_ref[rs, :] = xm
        else:
            xm = xm_ref[rs, :]
        hg = _dot(xm, wg_ref[...])
        hu = _dot(xm, wu_ref[...])
        yield
        y = _dot((hg * jax.nn.sigmoid(hg) * hu).astype(BF16), wo_ref[...])
        yield
        if final:
            if not first:
                y = o_ref[rs, :] + y
            z = DEEPNORM_ALPHA * x_ref[rs, :] + gt_ref[0] * (0.5 * y)
            o_ref[rs, :] = _layer_norm(z, lg_ref[...], lb_ref[...])
        elif first:
            o_ref[rs, :] = y
        else:
            o_ref[rs, :] += y

    def body(first, final):
        split = (first or final) and tm % 16 == 0
        rows = tm // 2 if split else tm
        _run_chains([chain(slice(r0, r0 + rows), first, final) for r0 in range(0, tm, rows)], 1)

    if nj == 1:
        body(True, True)
    else:
        pl.when(j == 0)(functools.partial(body, True, False))
        pl.when(j == nj - 1)(functools.partial(body, False, True))
        if nj > 2:
            pl.when((j > 0) & (j < nj - 1))(functools.partial(body, False, False))


def _ffn_call(x, mod_rows, mods, k0, wi, wo, lg, lb, tm_want=1024, tf_want=512):
    n, d = x.shape
    dff = wo.shape[0]
    tm = mods.tile(n, tm_want)
    tf = _row_tile(dff, tf_want)
    nj = dff // tf
    rot = functools.partial(mods.row, tm)
    return pl.pallas_call(
        functools.partial(_ffn_kernel, nj=nj),
        out_shape=jax.ShapeDtypeStruct((n, d), F32),
        grid=(n // tm, nj),
        in_specs=[pl.BlockSpec((tm, d), lambda i, j: (i, 0)),
                  _mod_spec(d, rot, k0), _mod_spec(d, rot, k0 + 1), _mod_spec(d, rot, k0 + 2),
                  pl.BlockSpec((d, tf), lambda i, j: (0, j)),
                  pl.BlockSpec((d, tf), lambda i, j: (0, j + nj)),
                  pl.BlockSpec((tf, d), lambda i, j: (j, 0)),
                  pl.BlockSpec((1, d), lambda i, j: (0, 0)),
                  pl.BlockSpec((1, d), lambda i, j: (0, 0))],
        out_specs=pl.BlockSpec((tm, d), lambda i, j: (i, 0)),
        scratch_shapes=[pltpu.VMEM((tm, d), BF16)],
        compiler_params=_params("parallel", "arbitrary"),
        name="ffn",
    )(x, mod_rows, mod_rows, mod_rows, wi, wi, wo, lg.reshape(1, d), lb.reshape(1, d))


def _proj_kernel(x_ref, sh_ref, sc_ref, w_ref, o_ref, xm_ref):
    @pl.when(pl.program_id(1) == 0)
    def _():
        xm_ref[...] = (x_ref[...] * (1.0 + sc_ref[0]) + sh_ref[0]).astype(BF16)

    o_ref[...] = _dot(xm_ref[...], w_ref[...])


def _proj_call(x, mod_rows, mods, k0, w, tm_want=1024, tn_want=1024):
    n, d = x.shape
    nout = w.shape[1]
    tm = mods.tile(n, tm_want)
    tn = tn_want if nout % tn_want == 0 else (nout // 2 if nout % 256 == 0 and nout > tn_want else nout)
    rot = functools.partial(mods.row, tm)
    return pl.pallas_call(
        _proj_kernel,
        out_shape=jax.ShapeDtypeStruct((n, nout), F32),
        grid=(n // tm, nout // tn),
        in_specs=[pl.BlockSpec((tm, d), lambda i, j: (i, 0)),
                  _mod_spec(d, rot, k0), _mod_spec(d, rot, k0 + 1),
                  pl.BlockSpec((d, tn), lambda i, j: (0, j))],
        out_specs=pl.BlockSpec((tm, tn), lambda i, j: (i, j)),
        scratch_shapes=[pltpu.VMEM((tm, d), BF16)],
        compiler_params=_params("parallel", "arbitrary"),
        name="proj",
    )(x, mod_rows, mod_rows, w)


def _pool_kernel(z_ref, w_ref, s_ref, o_ref):
    g = pl.program_id(1)
    u = z_ref[...]
    t_len = u.shape[0]
    t = lax.broadcasted_iota(jnp.int32, u.shape, 0)
    for gi, win in enumerate(POOL_WINDOWS):
        @pl.when(g == gi)
        def _(win=win):
            half = win // 2
            acc = u
            for dlt in range(-half, half):
                if dlt == 0:
                    continue
                shifted = pltpu.roll(u, (-dlt) % t_len, 0)
                ok = (t + dlt >= 0) & (t + dlt < t_len)
                acc = acc + jnp.where(ok, shifted, 0.0)
            cnt = (jnp.minimum(t + half, t_len) - jnp.maximum(t - half, 0)).astype(F32)
            pooled = acc / cnt - u
            y = _dot(pooled.astype(BF16), w_ref[0]) * s_ref[...]
            o_ref[...] = y.astype(o_ref.dtype)


def _pool_call(z, nseq, w_bf16, scale):
    n, c = z.shape
    t_len = n // nseq
    ng = len(POOL_WINDOWS)
    return pl.pallas_call(
        _pool_kernel,
        out_shape=jax.ShapeDtypeStruct((n, c), BF16),
        grid=(nseq, ng),
        in_specs=[pl.BlockSpec((t_len, POOL_GROUP), lambda b, g: (b, g)),
                  pl.BlockSpec((1, POOL_GROUP, POOL_GROUP), lambda b, g: (g, 0, 0)),
                  pl.BlockSpec((1, POOL_GROUP), lambda b, g: (0, g))],
        out_specs=pl.BlockSpec((t_len, POOL_GROUP), lambda b, g: (b, g)),
        compiler_params=_params("parallel", "arbitrary"),
        name="pool",
    )(z, w_bf16, scale.reshape(1, c))


def _rwkv_prep_kernel(z_ref, zp_ref, zn_ref, mu_ref, w0_ref, w2_ref, a0_ref, a2_ref, g2_ref,
                      kkp_ref, ka_ref, bd_ref,
                      r_ref, v_ref, kk_ref, g_ref, k_ref, b_ref, lw_ref, *, tiles_per_seq):
    i = pl.program_id(0)
    cur = z_ref[...]
    tm = cur.shape[0]
    c = BRANCH_W
    row = lax.broadcasted_iota(jnp.int32, cur.shape, 0)
    pos = i % tiles_per_seq
    prev_edge = jnp.where(pos == 0, 0.0, zp_ref[7:8, :])
    next_edge = jnp.where(pos == tiles_per_seq - 1, 0.0, zn_ref[0:1, :])
    prev = jnp.where(row == 0, prev_edge, pltpu.roll(cur, 1, 0))
    nxt = jnp.where(row == tm - 1, next_edge, pltpu.roll(cur, tm - 1, 0))
    zs = cur + (0.5 * (prev + nxt) - cur) * mu_ref[...]

    r = zs[:, :c]
    k = zs[:, c:2 * c]
    v = zs[:, 2 * c:3 * c]
    o = 3 * c
    wd = zs[:, o:o + 2 * LORA_PAD]
    ad = zs[:, o + 2 * LORA_PAD:o + 4 * LORA_PAD]
    gd = zs[:, o + 4 * LORA_PAD:]
    z_w = w0_ref[...] + _dot(jnp.tanh(wd).astype(BF16), w2_ref[...])
    lw = -DECAY_SCALE * jax.nn.sigmoid(z_w)
    a = jax.nn.sigmoid(a0_ref[...] + _dot(ad.astype(BF16), a2_ref[...]))
    g = _dot(jax.nn.sigmoid(gd).astype(BF16), g2_ref[...])
    kk = k * kkp_ref[...]
    ssq = _dot_exact_rhs(kk * kk, bd_ref[...])
    kk = kk * lax.rsqrt(ssq + 1e-12)
    r_ref[...] = r.astype(r_ref.dtype)
    v_ref[...] = v.astype(v_ref.dtype)
    kk_ref[...] = kk.astype(kk_ref.dtype)
    g_ref[...] = g.astype(g_ref.dtype)
    for dr in range(2):
        a_d = a[:, dr * c:(dr + 1) * c]
        k_ref[dr] = (k * (1.0 + (a_d - 1.0) * ka_ref[...])).astype(k_ref.dtype)
        b_ref[dr] = (kk * a_d).astype(b_ref.dtype)
        lw_ref[dr] = lw[:, dr * c:(dr + 1) * c]


def _rwkv_prep_call(z, t_len, pw, tm_want=256):
    n, w = z.shape
    c = BRANCH_W
    tm = _row_tile(t_len, tm_want)
    tps = t_len // tm
    nblk8 = n // 8
    full = lambda shape: pl.BlockSpec(shape, lambda i: (0,) * len(shape))
    tile = pl.BlockSpec((tm, c), lambda i: (i, 0))
    tile2 = pl.BlockSpec((2, tm, c), lambda i: (0, i, 0))
    one = jax.ShapeDtypeStruct((n, c), BF16)
    two = jax.ShapeDtypeStruct((2, n, c), BF16)
    logw = jax.ShapeDtypeStruct((2, n, c), F32)
    return pl.pallas_call(
        functools.partial(_rwkv_prep_kernel, tiles_per_seq=tps),
        out_shape=(one, one, one, one, two, two, logw),
        grid=(n // tm,),
        in_specs=[pl.BlockSpec((tm, w), lambda i: (i, 0)),
                  pl.BlockSpec((8, w), lambda i: (jnp.maximum(i * (tm // 8) - 1, 0), 0)),
                  pl.BlockSpec((8, w), lambda i: (jnp.minimum((i + 1) * (tm // 8), nblk8 - 1), 0)),
                  full((1, w)), full((1, 2 * c)), full((2 * LORA_PAD, 2 * c)),
                  full((1, 2 * c)), full((2 * LORA_PAD, 2 * c)), full((GATE_LORA, c)),
                  full((1, c)), full((1, c)), full((c, c))],
        out_specs=(tile, tile, tile, tile, tile2, tile2, tile2),
        compiler_params=_params("parallel"),
        name="rwkv_prep",
    )(z, z, z, pw["mu"], pw["w0"], pw["w2"], pw["a0"], pw["a2"], pw["g2"],
      pw["kk"], pw["ka"], pw["bd"])


def _block_diag4(x_bf16, head_mask):
    return jnp.concatenate([x_bf16] * 4, axis=0) * head_mask


def _dot_bd(a, x, head_mask):
    return _dot(a.astype(BF16), _block_diag4(x.astype(BF16), head_mask))


def _scan_masks(rev, cl):
    hw = RWKV_HEAD
    gw = 4 * hw
    row = lax.broadcasted_iota(jnp.int32, (cl, gw), 0)
    lane = lax.broadcasted_iota(jnp.int32, (cl, gw), 1)
    pos_t = (cl - 1 - row) if rev else row
    pos_s = lane % hw
    pos_s = (cl - 1 - pos_s) if rev else pos_s
    nlev = cl.bit_length() - 1
    brow = lax.broadcasted_iota(jnp.int32, (gw, gw), 0) // hw
    blane = lax.broadcasted_iota(jnp.int32, (gw, gw), 1) // hw
    ti = lax.broadcasted_iota(jnp.int32, (cl, cl), 0)
    si = lax.broadcasted_iota(jnp.int32, (cl, cl), 1)
    return {
        "lane_head": lane // hw,
        "strict": pos_s < pos_t,
        "incl": pos_s <= pos_t,
        "eye": jnp.where(pos_s == pos_t, 1.0, 0.0),
        "levels": [((pos_t >> (lev + 1)) == (pos_s >> (lev + 1)))
                   & (((pos_t >> lev) & 1) == 1) & (((pos_s >> lev) & 1) == 0)
                   for lev in range(nlev)],
        "head": jnp.where(brow == blane, 1.0, 0.0).astype(BF16),
        "tri": jnp.where((si >= ti) if rev else (si <= ti), 1.0, 0.0).astype(BF16),
    }


def _scan_group(mk, ab, qb, kb, bb, kh, bh, vb, g_tot, s_ref, sl, out):
    cl = ab.shape[0]
    hw = RWKV_HEAD
    hm = mk["head"]
    aq = jnp.concatenate([ab, qb], axis=0)
    p_b = _dot(aq, _block_diag4(bb, hm), _NT)
    p_k = _dot(aq, _block_diag4(kb, hm), _NT)
    n_ab = jnp.where(mk["strict"], p_b[:cl], 0.0)
    a_ak = jnp.where(mk["strict"], p_k[:cl], 0.0).astype(BF16)
    a_qb = jnp.where(mk["incl"], p_b[cl:], 0.0).astype(BF16)
    a_qk = jnp.where(mk["incl"], p_k[cl:], 0.0).astype(BF16)
    yield
    t_inv = mk["eye"] - jnp.where(mk["levels"][0], n_ab, 0.0)
    for lev in range(1, len(mk["levels"])):
        x = _dot_bd(jnp.where(mk["levels"][lev], n_ab, 0.0), t_inv, hm)
        yield
        t_inv = t_inv - _dot_bd(t_inv, x, hm)
        yield
    s_g = s_ref[:, sl]
    p_s = _dot(aq, _block_diag4(s_g.astype(BF16), hm), _NT)
    p_v = _dot(jnp.concatenate([a_ak, a_qk], axis=0), _block_diag4(vb, hm))
    yield
    ub = (-_dot_bd(t_inv, p_s[:cl] + p_v[:cl], hm)).astype(BF16)
    yield
    out.append(p_s[cl:] + p_v[cl:] + _dot(a_qb, _block_diag4(ub, hm)))
    full = _dot(jnp.concatenate([ub, vb], axis=0), jnp.concatenate([bh, kh], axis=0), _TN)
    upd = jnp.where(mk["lane_head"] == 0, full[:hw], 0.0)
    for h in range(1, 4):
        upd = upd + jnp.where(mk["lane_head"] == h, full[h * hw:(h + 1) * hw], 0.0)
    s_ref[:, sl] = s_g * g_tot + upd


def _scan_direction(mk, r, v, kk, k, b, lw, s_ref, out):
    cl, c = lw.shape
    gw = 4 * RWKV_HEAD
    assert cl == RWKV_HEAD
    cum = _dot_exact_lhs(mk["tri"], lw)
    tot = jnp.sum(lw, axis=0, keepdims=True)
    g_inv = jnp.exp(-cum)
    g_end = jnp.exp(tot - cum)
    g_tot = jnp.exp(tot)
    qb = (r * jnp.exp(cum)).astype(BF16)
    ab = (kk * jnp.exp(cum - lw)).astype(BF16)
    kb = (k * g_inv).astype(BF16)
    bb = (b * g_inv).astype(BF16)
    kh = (k * g_end).astype(BF16)
    bh = (b * g_end).astype(BF16)
    vb = v.astype(BF16)
    gens = []
    for g in range(c // gw):
        sl = slice(g * gw, (g + 1) * gw)
        gens.append(_scan_group(mk, ab[:, sl], qb[:, sl], kb[:, sl], bb[:, sl], kh[:, sl], bh[:, sl],
                                vb[:, sl], g_tot[:, sl], s_ref, sl, out))
    return gens


SCAN_ROWS_PER_STEP = 4


def _scan_kernel(rf_ref, rb_ref, vf_ref, vb_ref, kkf_ref, kkb_ref, kf_ref, kb_ref, bf_ref, bb_ref,
                 lwf_ref, lwb_ref, s0_ref, of_ref, ob_ref, sfin_ref, s_ref):
    n = pl.program_id(1)

    @pl.when(n == 0)
    def _():
        s_ref[...] = s0_ref[...]

    nrow, cl, _ = rf_ref.shape
    mk_f = _scan_masks(False, cl)
    mk_b = _scan_masks(True, cl)
    gens, outs = [], []
    for i in range(nrow):
        out_f, out_b = [], []
        outs.append((out_f, out_b))
        gens += _scan_direction(mk_f, rf_ref[i], vf_ref[i], kkf_ref[i], kf_ref[i], bf_ref[i],
                                lwf_ref[i], s_ref.at[0, i], out_f)
        gens += _scan_direction(mk_b, rb_ref[i], vb_ref[i], kkb_ref[i], kb_ref[i], bb_ref[i],
                                lwb_ref[i], s_ref.at[1, i], out_b)
    _run_chains(gens)
    for i, (out_f, out_b) in enumerate(outs):
        of_ref[i] = jnp.concatenate(out_f, axis=1)
        ob_ref[i] = jnp.concatenate(out_b, axis=1)

    @pl.when(n == pl.num_programs(1) - 1)
    def _():
        sfin_ref[...] = s_ref[...]


def _scan_call(prep, s0, nbatch, t_len):
    r, v, kk, _, k2, b2, lw2 = prep
    n, c = r.shape
    cl = SCAN_CHUNK
    nc = t_len // cl
    nrow = math.gcd(nbatch, SCAN_ROWS_PER_STEP)
    seq = lambda a: a.reshape(a.shape[:-2] + (nbatch, t_len, c))
    fwd = pl.BlockSpec((nrow, cl, c), lambda bi, i: (bi, i, 0))
    bwd = pl.BlockSpec((nrow, cl, c), lambda bi, i: (bi, nc - 1 - i, 0))
    fwd2 = pl.BlockSpec((None, nrow, cl, c), lambda bi, i: (0, bi, i, 0))
    bwd2 = pl.BlockSpec((None, nrow, cl, c), lambda bi, i: (1, bi, nc - 1 - i, 0))
    state = pl.BlockSpec((2, nrow, RWKV_HEAD, c), lambda bi, i: (0, bi, 0, 0))
    o_f, o_b, s_fin = pl.pallas_call(
        _scan_kernel,
        out_shape=(jax.ShapeDtypeStruct((nbatch, t_len, c), F32),
                   jax.ShapeDtypeStruct((nbatch, t_len, c), F32),
                   jax.ShapeDtypeStruct(s0.shape, F32)),
        grid=(nbatch // nrow, nc),
        in_specs=[fwd, bwd, fwd, bwd, fwd, bwd, fwd2, bwd2, fwd2, bwd2, fwd2, bwd2, state],
        out_specs=(fwd, bwd, state),
        scratch_shapes=[pltpu.VMEM((2, nrow, RWKV_HEAD, c), F32)],
        compiler_params=_params("parallel", "arbitrary"),
        name="rwkv_scan",
    )(seq(r), seq(r), seq(v), seq(v), seq(kk), seq(kk), seq(k2), seq(k2), seq(b2), seq(b2),
      seq(lw2), seq(lw2), s0)
    return o_f.reshape(n, c), o_b.reshape(n, c), s_fin


def _rwkv_out_kernel(of_ref, ob_ref, r_ref, k_ref, v_ref, g_ref, rk_ref, gg_ref, gb_ref, bd_ref, y_ref):
    bd = bd_ref[...]
    inv_n = 1.0 / RWKV_HEAD
    s = of_ref[...] + ob_ref[...]
    mu = _dot_exact_rhs(s, bd) * inv_n
    sc = s - mu
    var = _dot_exact_rhs(sc * sc, bd) * inv_n
    y = sc * lax.rsqrt(var + GN_EPS) * gg_ref[...] + gb_ref[...]
    r = r_ref[...].astype(F32)
    k_sum = k_ref[0].astype(F32) + k_ref[1].astype(F32)
    bonus = _dot_exact_rhs(r * k_sum * rk_ref[...], bd)
    y_ref[...] = ((y + bonus * v_ref[...].astype(F32)) * g_ref[...].astype(F32)).astype(y_ref.dtype)


def _rwkv_out_call(o_f, o_b, prep, pw, tm_want=512):
    r, v, _, g, k2, _, _ = prep
    n, c = r.shape
    tm = _row_tile(n, tm_want)
    tile = pl.BlockSpec((tm, c), lambda i: (i, 0))
    tile2 = pl.BlockSpec((2, tm, c), lambda i: (0, i, 0))
    vec = pl.BlockSpec((1, c), lambda i: (0, 0))
    return pl.pallas_call(
        _rwkv_out_kernel,
        out_shape=jax.ShapeDtypeStruct((n, c), BF16),
        grid=(n // tm,),
        in_specs=[tile, tile, tile, tile2, tile, tile, vec, vec, vec,
                  pl.BlockSpec((c, c), lambda i: (0, 0))],
        out_specs=tile,
        compiler_params=_params("parallel"),
        name="rwkv_out",
    )(o_f, o_b, r, k2, v, g, pw["rk"], pw["gn_g"], pw["gn_b"], pw["bd"])


def _attn_prep_kernel(zw_ref, zd_ref, cos_ref, sin_ref, qn_ref, kn_ref,
                      qw_ref, kw_ref, vw_ref, qd_ref, kd_ref, vd_ref, *, rope):
    scale = HEAD_DIM ** -0.5
    hd = HEAD_DIM
    if rope:
        cosf = cos_ref[...]
        sins = sin_ref[...]
        lane = lax.broadcasted_iota(jnp.int32, cosf.shape, 1)
        first = (lane % (hd // 2)) < (hd // 4)

    def rot(xh):
        if not rope:
            return xh
        partner = jnp.where(first, pltpu.roll(xh, hd - hd // 4, 1), pltpu.roll(xh, hd // 4, 1))
        return xh * cosf + partner * sins

    def rms(xh, gain):
        return xh * lax.rsqrt(jnp.mean(xh * xh, axis=-1, keepdims=True) + RMS_EPS) * gain

    for h in range(Q_HEADS):
        sl = slice(h * hd, (h + 1) * hd)
        qw_ref[:, sl] = (rot(zw_ref[:, sl]) * scale).astype(BF16)
        qd_ref[:, sl] = (rot(rms(zd_ref[:, sl], qn_ref[...])) * scale).astype(BF16)
    for h in range(KV_HEADS):
        sl = slice(h * hd, (h + 1) * hd)
        ks = slice((Q_HEADS + h) * hd, (Q_HEADS + h + 1) * hd)
        vs = slice((Q_HEADS + KV_HEADS + h) * hd, (Q_HEADS + KV_HEADS + h + 1) * hd)
        kw_ref[:, sl] = rot(zw_ref[:, ks]).astype(BF16)
        kd_ref[:, sl] = rot(rms(zd_ref[:, ks], kn_ref[...])).astype(BF16)
        vw_ref[:, sl] = zw_ref[:, vs].astype(BF16)
        vd_ref[:, sl] = zd_ref[:, vs].astype(BF16)


def _attn_prep_call(zw, zd, t_len, cosf, sins, qn, kn, rope, tm_want=256):
    n = zw.shape[0]
    tm = _row_tile(t_len, tm_want)
    tps = t_len // tm
    hd = HEAD_DIM
    tile = lambda w: pl.BlockSpec((tm, w), lambda i: (i, 0))
    pos = pl.BlockSpec((tm, hd), lambda i: (i % tps, 0))
    vec = pl.BlockSpec((1, hd), lambda i: (0, 0))
    qs = jax.ShapeDtypeStruct((n, Q_HEADS * hd), BF16)
    ks = jax.ShapeDtypeStruct((n, KV_HEADS * hd), BF16)
    return pl.pallas_call(
        functools.partial(_attn_prep_kernel, rope=rope),
        out_shape=(qs, ks, ks, qs, ks, ks),
        grid=(n // tm,),
        in_specs=[tile(ATT_W), tile(ATT_W), pos, pos, vec, vec],
        out_specs=(tile(Q_HEADS * hd), tile(KV_HEADS * hd), tile(KV_HEADS * hd),
                   tile(Q_HEADS * hd), tile(KV_HEADS * hd), tile(KV_HEADS * hd)),
        compiler_params=_params("parallel"),
        name="attn_prep",
    )(zw, zd, cosf, sins, qn.reshape(1, hd), kn.reshape(1, hd))


def _full_attn_kernel(q_ref, k_ref, v_ref, *rest, sink):
    if sink:
        sink_ref, o_ref = rest
    else:
        (o_ref,) = rest
    hd = HEAD_DIM
    k = k_ref[...]
    v = v_ref[...]
    group = Q_HEADS // KV_HEADS
    tq = q_ref.shape[0]
    rows = min(tq, FULL_ATTN_CHAIN_ROWS)

    def chain(g, r0):
        sl = slice(g * hd, (g + 1) * hd)
        rs = slice(r0, r0 + rows)
        s = _dot(q_ref[rs, sl], k, _NT)
        yield
        m = jnp.max(s, axis=-1, keepdims=True)
        if sink:
            sk = sink_ref[:, g * hd:g * hd + 1]
            m = jnp.maximum(m, sk)
        p = jnp.exp(s - m)
        den = jnp.sum(p, axis=-1, keepdims=True)
        if sink:
            den = den + jnp.exp(sk - m)
        yield
        o_ref[rs, sl] = (_dot(p.astype(BF16), v) / den).astype(o_ref.dtype)

    _run_chains([chain(g, r0) for g in range(group) for r0 in range(0, tq, rows)], 1)


FULL_ATTN_CHAIN_ROWS = 128


def _full_attn_call(q, k, v, nbatch, sink_row=None, tq_want=512):
    n = q.shape[0]
    hd = HEAD_DIM
    tq_len = n // nbatch
    tk_len = k.shape[0] // nbatch
    tq = _row_tile(tq_len, tq_want)
    nq = tq_len // tq
    gw = (Q_HEADS // KV_HEADS) * hd
    in_specs = [pl.BlockSpec((tq, gw), lambda b, h, i: (b * nq + i, h)),
                pl.BlockSpec((tk_len, hd), lambda b, h, i: (b, h)),
                pl.BlockSpec((tk_len, hd), lambda b, h, i: (b, h))]
    args = [q, k, v]
    if sink_row is not None:
        in_specs.append(pl.BlockSpec((1, gw), lambda b, h, i: (0, h)))
        args.append(sink_row)
    return pl.pallas_call(
        functools.partial(_full_attn_kernel, sink=sink_row is not None),
        out_shape=jax.ShapeDtypeStruct((n, Q_HEADS * hd), BF16),
        grid=(nbatch, KV_HEADS, nq),
        in_specs=in_specs,
        out_specs=pl.BlockSpec((tq, gw), lambda b, h, i: (b * nq + i, h)),
        compiler_params=_params("parallel", "parallel", "arbitrary"),
        name="full_attn",
    )(*args)


WIN_BLOCKS_PER_STEP = 8


def _win_attn_kernel(q_ref, kx_ref, vx_ref, kc_ref, vc_ref, sink_ref, o_ref):
    hd = HEAD_DIM
    step = pl.program_id(2)
    t_len = kx_ref.shape[0]
    nsub = q_ref.shape[0] // BLOCK
    span = 3 * BLOCK
    kc = kc_ref[...]
    vc = vc_ref[...]
    l_len = kc.shape[0]
    group = Q_HEADS // KV_HEADS

    def block_keys(j):
        nblk = step * nsub + j
        start = pl.multiple_of(jnp.clip((nblk - 1) * BLOCK, 0, t_len - span), BLOCK)
        k_all = jnp.concatenate([kc, kx_ref[pl.ds(start, span), :]], axis=0)
        v_all = jnp.concatenate([vc, vx_ref[pl.ds(start, span), :]], axis=0)
        col = lax.broadcasted_iota(jnp.int32, (BLOCK, l_len + span), 1)
        qpos = nblk * BLOCK + lax.broadcasted_iota(jnp.int32, (BLOCK, l_len + span), 0)
        allowed = (col < l_len) | (jnp.abs(qpos - (start + col - l_len)) <= WIN)
        return k_all, v_all, allowed

    def chain(j, g, keys):
        k_all, v_all, allowed = keys
        rs = slice(j * BLOCK, (j + 1) * BLOCK)
        sl = slice(g * hd, (g + 1) * hd)
        s = _dot(q_ref[rs, sl], k_all, _NT)
        yield
        s = jnp.where(allowed, s, NEG_INF)
        sk = sink_ref[:, g * hd:g * hd + 1]
        m = jnp.maximum(jnp.max(s, axis=-1, keepdims=True), sk)
        p = jnp.exp(s - m)
        den = jnp.sum(p, axis=-1, keepdims=True) + jnp.exp(sk - m)
        yield
        o_ref[rs, sl] = (_dot(p.astype(BF16), v_all) / den).astype(o_ref.dtype)

    chains = []
    for j in range(nsub):
        keys = block_keys(j)
        chains += [chain(j, g, keys) for g in range(group)]
    _run_chains(chains, 4)


def _win_attn_call(q, kx, vx, kc, vc, sink_row, nbatch):
    n = q.shape[0]
    hd = HEAD_DIM
    t_len = n // nbatch
    l_len = kc.shape[0] // nbatch
    tq = _row_tile(t_len, WIN_BLOCKS_PER_STEP * BLOCK)
    nq = t_len // tq
    gw = (Q_HEADS // KV_HEADS) * hd
    return pl.pallas_call(
        _win_attn_kernel,
        out_shape=jax.ShapeDtypeStruct((n, Q_HEADS * hd), BF16),
        grid=(nbatch, KV_HEADS, nq),
        in_specs=[pl.BlockSpec((tq, gw), lambda b, h, i: (b * nq + i, h)),
                  pl.BlockSpec((t_len, hd), lambda b, h, i: (b, h)),
                  pl.BlockSpec((t_len, hd), lambda b, h, i: (b, h)),
                  pl.BlockSpec((l_len, hd), lambda b, h, i: (b, h)),
                  pl.BlockSpec((l_len, hd), lambda b, h, i: (b, h)),
                  pl.BlockSpec((1, gw), lambda b, h, i: (0, h))],
        out_specs=pl.BlockSpec((tq, gw), lambda b, h, i: (b * nq + i, h)),
        compiler_params=_params("parallel", "parallel", "arbitrary"),
        name="win_attn",
    )(q, kx, vx, kc, vc, sink_row)


MERGE_COL_SPLIT = 2


def _merge_kernel(x_ref, sh_ref, sc_ref, gt_ref, b0_ref, b1_ref, b2_ref, b3_ref, wg_ref, bg_ref,
                  wup_ref, wout_ref, lg_ref, lb_ref, o_ref, hm_ref, acc_ref):
    kb = pl.program_id(1)
    tn = wg_ref.shape[1]

    @pl.when(kb == 0)
    def _():
        hm_ref[...] = (x_ref[...] * (1.0 + sc_ref[0]) + sh_ref[0]).astype(BF16)

    for i, br in enumerate((b0_ref, b1_ref, b2_ref, b3_ref)):
        for h in range(MERGE_COL_SPLIT):
            @pl.when(kb == i * MERGE_COL_SPLIT + h)
            def _(i=i, br=br, h=h):
                cols = slice(h * tn, (h + 1) * tn)
                gate = jax.nn.sigmoid(_dot(hm_ref[...], wg_ref[...]) + bg_ref[...])
                y = gate * _dot(br[...], wup_ref[0])
                if i == 0:
                    acc_ref[:, cols] = y
                else:
                    acc_ref[:, cols] += y

    @pl.when(kb == pl.num_programs(1) - 1)
    def _():
        out = _dot(acc_ref[...].astype(BF16), wout_ref[...])
        y = DEEPNORM_ALPHA * x_ref[...] + gt_ref[0] * out
        o_ref[...] = _layer_norm(y, lg_ref[...], lb_ref[...])


def _merge_call(x, mod_rows, mods, branches, w_gate, b_gate, wup, wout, lg, lb,
                tm_want=512):
    n, d = x.shape
    c = BRANCH_W
    tm = mods.tile(n, tm_want)
    tn = d // MERGE_COL_SPLIT
    rot = functools.partial(mods.row, tm)
    br_spec = pl.BlockSpec((tm, c), lambda i, j: (i, 0))
    return pl.pallas_call(
        _merge_kernel,
        out_shape=jax.ShapeDtypeStruct((n, d), F32),
        grid=(n // tm, N_BRANCH * MERGE_COL_SPLIT),
        in_specs=[pl.BlockSpec((tm, d), lambda i, j: (i, 0)),
                  _mod_spec(d, rot, 3), _mod_spec(d, rot, 4), _mod_spec(d, rot, 5),
                  br_spec, br_spec, br_spec, br_spec,
                  pl.BlockSpec((d, tn), lambda i, j: (0, j)),
                  pl.BlockSpec((1, tn), lambda i, j: (0, j)),
                  pl.BlockSpec((1, c, tn), lambda i, j: (j // MERGE_COL_SPLIT, 0, j % MERGE_COL_SPLIT)),
                  pl.BlockSpec((d, d), lambda i, j: (0, 0), pipeline_mode=pl.Buffered(1)),
                  pl.BlockSpec((1, d), lambda i, j: (0, 0)),
                  pl.BlockSpec((1, d), lambda i, j: (0, 0))],
        out_specs=pl.BlockSpec((tm, d), lambda i, j: (i, 0)),
        scratch_shapes=[pltpu.VMEM((tm, d), BF16), pltpu.VMEM((tm, d), F32)],
        compiler_params=_params("parallel", "arbitrary"),
        name="merge",
    )(x, mod_rows, mod_rows, mod_rows, *branches, w_gate, b_gate.reshape(1, N_BRANCH * d), wup, wout,
      lg.reshape(1, d), lb.reshape(1, d))


def _pad_rows(w, rows):
    return jnp.pad(w, ((0, rows - w.shape[0]), (0, 0)))


def _rwkv_weights(w_rwkv, mu, w0, w2, a0, a2, g2, kk, ka, rk, gn_g, gn_b):
    c = BRANCH_W
    o = 3 * c
    cuts = [o, o + DECAY_LORA, o + 2 * DECAY_LORA, o + 2 * DECAY_LORA + ICLR_LORA,
            o + 2 * DECAY_LORA + 2 * ICLR_LORA]

    def pad_cols(m):
        parts = [m[:, :o]]
        for lo, hi in zip(cuts[:-1], cuts[1:]):
            parts.append(jnp.pad(m[:, lo:hi], ((0, 0), (0, LORA_PAD - (hi - lo)))))
        parts.append(m[:, cuts[-1]:])
        return jnp.concatenate(parts, axis=1)

    def block_diag(m):
        z = jnp.zeros((LORA_PAD, c), m.dtype)
        top = jnp.concatenate([_pad_rows(m[0], LORA_PAD), z], axis=1)
        bot = jnp.concatenate([z, _pad_rows(m[1], LORA_PAD)], axis=1)
        return jnp.concatenate([top, bot], axis=0)

    head = jnp.arange(c) // RWKV_HEAD
    return {
        "w": pad_cols(w_rwkv).astype(BF16),
        "mu": pad_cols(mu.reshape(1, -1)),
        "w0": w0.reshape(1, 2 * c), "w2": block_diag(w2).astype(BF16),
        "a0": a0.reshape(1, 2 * c), "a2": block_diag(a2).astype(BF16),
        "g2": g2.astype(BF16),
        "kk": kk.reshape(1, c), "ka": ka.reshape(1, c), "rk": rk.reshape(1, c),
        "gn_g": gn_g.reshape(1, c), "gn_b": gn_b.reshape(1, c),
        "bd": (head[:, None] == head[None, :]).astype(BF16),
    }


def _rope_tables(t_len):
    rows = t_len // GRID_W
    row = jnp.repeat(jnp.arange(rows), GRID_W).astype(F32)
    col = (jnp.arange(t_len) % GRID_W).astype(F32)
    n_freq = HEAD_DIM // 4
    inv = ROPE_THETA ** (-jnp.arange(n_freq, dtype=F32) / n_freq)
    ar = row[:, None] * inv
    ac = col[:, None] * inv
    cosf = jnp.concatenate([jnp.cos(ar), jnp.cos(ar), jnp.cos(ac), jnp.cos(ac)], axis=1)
    sins = jnp.concatenate([-jnp.sin(ar), jnp.sin(ar), -jnp.sin(ac), jnp.sin(ac)], axis=1)
    return cosf, sins


def kernel(x, c, ctx, c_ctx, w_mod, b_mod, ln_g, ln_b, ffn1_wi, ffn1_wo, ffn2_wi, ffn2_wo, w_in, b_gate, pool_w, pool_scale, rwkv_mu, rwkv_w0, rwkv_w2, rwkv_a0, rwkv_a2, rwkv_g2, rwkv_kk, rwkv_ka, rwkv_rk, rwkv_gn_g, rwkv_gn_b, c_sink, d_qnorm, d_knorm, w_up, w_out):
    nb, t_len, d = x.shape
    l_len = ctx.shape[1]
    depth = w_mod.shape[0]
    cw = BRANCH_W
    assert depth == DEPTH and t_len % GRID_W == 0 and t_len >= 3 * BLOCK

    mod_pad = 16
    c_all = jnp.zeros((mod_pad, d), F32).at[:nb].set(c).at[nb].set(c_ctx)
    cosf, sins = _rope_tables(t_len)
    s0 = jnp.zeros((2, nb, RWKV_HEAD, BRANCH_W), F32)

    row_x = _ModIndex(0, t_len)
    row_c = _ModIndex(nb, nb * l_len)

    xs = x.reshape(nb * t_len, d)
    xc = ctx.reshape(nb * l_len, d)
    o1 = cw
    o2 = o1 + 3 * cw + 2 * DECAY_LORA + 2 * ICLR_LORA + GATE_LORA
    o3 = o2 + ATT_W
    o4 = o3 + ATT_W

    for l in range(depth):
        last = l == depth - 1
        mod_rows = _mod_call(c_all, w_mod[l], b_mod[l]).reshape(mod_pad * N_MOD, 1, d)
        wi1, wo1 = ffn1_wi[l].astype(BF16), ffn1_wo[l].astype(BF16)
        wi2, wo2 = ffn2_wi[l].astype(BF16), ffn2_wo[l].astype(BF16)
        w_pool = w_in[l][:, :o1].astype(BF16)
        pw = _rwkv_weights(w_in[l][:, o1:o2], rwkv_mu[l], rwkv_w0[l], rwkv_w2[l], rwkv_a0[l],
                           rwkv_a2[l], rwkv_g2[l], rwkv_kk[l], rwkv_ka[l], rwkv_rk[l],
                           rwkv_gn_g[l], rwkv_gn_b[l])
        w_win = w_in[l][:, o2:o3].astype(BF16)
        w_dense = w_in[l][:, o3:o4].astype(BF16)
        w_gate = w_in[l][:, o4:].astype(BF16)
        pool_wb = pool_w[l].astype(BF16)
        wup = w_up[l].astype(BF16)
        wout = w_out[l].astype(BF16)
        sink_row = jnp.repeat(c_sink[l], HEAD_DIM).reshape(1, Q_HEADS * HEAD_DIM)

        xs = _ffn_call(xs, mod_rows, row_x, 0, wi1, wo1, ln_g[l, 0], ln_b[l, 0])
        xc = _ffn_call(xc, mod_rows, row_c, 0, wi1, wo1, ln_g[l, 0], ln_b[l, 0])

        proj_x = lambda w: _proj_call(xs, mod_rows, row_x, 3, w)
        proj_c = lambda w: _proj_call(xc, mod_rows, row_c, 3, w)

        prep_c = _rwkv_prep_call(proj_c(pw["w"]), l_len, pw)
        prep_x = _rwkv_prep_call(proj_x(pw["w"]), t_len, pw)
        oc_f, oc_b, s_ctx = _scan_call(prep_c, s0, nb, l_len)
        ox_f, ox_b, _ = _scan_call(prep_x, s_ctx, nb, t_len)
        yb_x = _rwkv_out_call(ox_f, ox_b, prep_x, pw)

        qw_c, kw_c, vw_c, qd_c, kd_c, vd_c = _attn_prep_call(
            proj_c(w_win), proj_c(w_dense), l_len, cosf, sins, d_qnorm[l], d_knorm[l], rope=False)
        qw_x, kw_x, vw_x, qd_x, kd_x, vd_x = _attn_prep_call(
            proj_x(w_win), proj_x(w_dense), t_len, cosf, sins, d_qnorm[l], d_knorm[l], rope=True)
        yc_x = _win_attn_call(qw_x, kw_x, vw_x, kw_c, vw_c, sink_row, nb)
        kv_w = KV_HEADS * HEAD_DIM
        kd_all = jnp.concatenate([kd_c.reshape(nb, l_len, kv_w), kd_x.reshape(nb, t_len, kv_w)],
                                 axis=1).reshape(nb * (l_len + t_len), kv_w)
        vd_all = jnp.concatenate([vd_c.reshape(nb, l_len, kv_w), vd_x.reshape(nb, t_len, kv_w)],
                                 axis=1).reshape(nb * (l_len + t_len), kv_w)
        yd_x = _full_attn_call(qd_x, kd_all, vd_all, nb)

        ya_x = _pool_call(proj_x(w_pool), nb, pool_wb, pool_scale[l])

        xs = _merge_call(xs, mod_rows, row_x, (ya_x, yb_x, yc_x, yd_x), w_gate, b_gate[l], wup, wout,
                         ln_g[l, 1], ln_b[l, 1])
        xs = _ffn_call(xs, mod_rows, row_x, 6, wi2, wo2, ln_g[l, 2], ln_b[l, 2])

        if not last:
            ya_c = _pool_call(proj_c(w_pool), nb, pool_wb, pool_scale[l])
            yb_c = _rwkv_out_call(oc_f, oc_b, prep_c, pw)
            yc_c = _full_attn_call(qw_c, kw_c, vw_c, nb, sink_row)
            yd_c = _full_attn_call(qd_c, kd_c, vd_c, nb)
            xc = _merge_call(xc, mod_rows, row_c, (ya_c, yb_c, yc_c, yd_c), w_gate, b_gate[l], wup, wout,
                             ln_g[l, 1], ln_b[l, 1])
            xc = _ffn_call(xc, mod_rows, row_c, 6, wi2, wo2, ln_g[l, 2], ln_b[l, 2])

    return xs.reshape(nb, t_len, d)
```

```python
import functools
import math

import jax
import jax.numpy as jnp
from jax import lax
from jax.experimental import pallas as pl
from jax.experimental.pallas import tpu as pltpu

F32 = jnp.float32
BF16 = jnp.bfloat16

GRID_W = 64
HEAD_DIM = 128
N_BRANCH = 4
BRANCH_W = 512
POOL_WINDOWS = (2, 4, 8, 16)
POOL_GROUP = BRANCH_W // len(POOL_WINDOWS)
RWKV_HEAD = 64
DECAY_LORA = 96
ICLR_LORA = 96
GATE_LORA = 256
LORA_PAD = 128
WIN = 128
BLOCK = 128
Q_HEADS = 4
KV_HEADS = 2
ATT_W = (Q_HEADS + 2 * KV_HEADS) * HEAD_DIM
N_MOD = 9
DEPTH = 2
ROPE_THETA = 10000.0
DEEPNORM_ALPHA = (2 * DEPTH) ** 0.25
LN_EPS = 1e-6
RMS_EPS = 1e-6
GN_EPS = 64e-5
NEG_INF = -1e30
DECAY_SCALE = math.exp(-0.5)

V7X_VMEM_BYTES = 64 * 1024 * 1024
VMEM_LIMIT = V7X_VMEM_BYTES - 8 * 1024 * 1024
SCAN_CHUNK = 64

_NT = (((1,), (1,)), ((), ()))
_TN = (((0,), (0,)), ((), ()))


def _params(*sem):
    return pltpu.CompilerParams(dimension_semantics=sem, vmem_limit_bytes=VMEM_LIMIT)


def _dot(a, b, dims=None):
    if dims is None:
        return jnp.dot(a, b, preferred_element_type=F32)
    return lax.dot_general(a, b, dims, preferred_element_type=F32)


def _dot_exact_lhs(a_bf16, b):
    b0 = b.astype(BF16)
    r1 = b - b0.astype(F32)
    b1 = r1.astype(BF16)
    b2 = (r1 - b1.astype(F32)).astype(BF16)
    return _dot(a_bf16, b0) + (_dot(a_bf16, b1) + _dot(a_bf16, b2))


def _dot_exact_rhs(a, b_bf16):
    a0 = a.astype(BF16)
    r1 = a - a0.astype(F32)
    a1 = r1.astype(BF16)
    a2 = (r1 - a1.astype(F32)).astype(BF16)
    return _dot(a0, b_bf16) + (_dot(a1, b_bf16) + _dot(a2, b_bf16))


def _layer_norm(y, g, b):
    mu = jnp.mean(y, axis=-1, keepdims=True)
    yc = y - mu
    var = jnp.mean(yc * yc, axis=-1, keepdims=True)
    return yc * lax.rsqrt(var + LN_EPS) * g + b


def _run_chains(gens, width=None):
    started, pending = [], list(gens)
    width = width or len(pending)
    while started or pending:
        started += pending[:width]
        pending = pending[width:]
        alive = []
        for gen in started:
            try:
                next(gen)
                alive.append(gen)
            except StopIteration:
                pass
        started = alive


def _row_tile(n, want):
    t = min(want, n)
    assert n % t == 0, (n, t)
    return t


def _mod_kernel(c_ref, w_ref, b_ref, o_ref):
    c = c_ref[...]
    s = c * jax.nn.sigmoid(c)
    o_ref[...] = _dot(s.astype(BF16), w_ref[...].astype(BF16)) + b_ref[...]


def _mod_call(c_all, w, b):
    rows, d = c_all.shape
    nout = w.shape[1]
    tn = _row_tile(nout, min(d, 1024))
    return pl.pallas_call(
        _mod_kernel,
        out_shape=jax.ShapeDtypeStruct((rows, nout), F32),
        grid=(nout // tn,),
        in_specs=[pl.BlockSpec((rows, d), lambda j: (0, 0)),
                  pl.BlockSpec((d, tn), lambda j: (0, j)),
                  pl.BlockSpec((1, tn), lambda j: (0, j))],
        out_specs=pl.BlockSpec((rows, tn), lambda j: (0, j)),
        compiler_params=_params("arbitrary"),
        name="mod",
    )(c_all, w, b.reshape(1, nout))


class _ModIndex:
    def __init__(self, base, rows_per_mod):
        self.base = base
        self.rows_per_mod = rows_per_mod

    def tile(self, n, want):
        return _row_tile(min(n, self.rows_per_mod), want)

    def row(self, tm, i):
        return self.base + (i * tm) // self.rows_per_mod


def _mod_spec(d, row_of_tile, k):
    return pl.BlockSpec((1, 1, d), lambda i, j: (row_of_tile(i) * N_MOD + k, 0, 0))


def _ffn_kernel(x_ref, sh_ref, sc_ref, gt_ref, wg_ref, wu_ref, wo_ref, lg_ref, lb_ref,
                o_ref, xm_ref, *, nj):
    j = pl.program_id(1)
    tm = x_ref.shape[0]

    def chain(rs, first, final):
        if first:
            xm = (x_ref[rs, :] * (1.0 + sc_ref[0]) + sh_ref[0]).astype(BF16)
            xm_ref[rs, :] = xm
        else:
            xm = xm_ref[rs, :]
        hg = _dot(xm, wg_ref[...])
        hu = _dot(xm, wu_ref[...])
        yield
        y = _dot((hg * jax.nn.sigmoid(hg) * hu).astype(BF16), wo_ref[...])
        yield
        if final:
            if not first:
                y = o_ref[rs, :] + y
            z = DEEPNORM_ALPHA * x_ref[rs, :] + gt_ref[0] * (0.5 * y)
            o_ref[rs, :] = _layer_norm(z, lg_ref[...], lb_ref[...])
        elif first:
            o_ref[rs, :] = y
        else:
            o_ref[rs, :] += y

    def body(first, final):
        split = (first or final) and tm % 16 == 0
        rows = tm // 2 if split else tm
        _run_chains([chain(slice(r0, r0 + rows), first, final) for r0 in range(0, tm, rows)], 1)

    if nj == 1:
        body(True, True)
    else:
        pl.when(j == 0)(functools.partial(body, True, False))
        pl.when(j == nj - 1)(functools.partial(body, False, True))
        if nj > 2:
            pl.when((j > 0) & (j < nj - 1))(functools.partial(body, False, False))


def _ffn_call(x, mod_rows, mods, k0, wi, wo, lg, lb, tm_want=1024, tf_want=512):
    n, d = x.shape
    dff = wo.shape[0]
    tm = mods.tile(n, tm_want)
    tf = _row_tile(dff, tf_want)
    nj = dff // tf
    rot = functools.partial(mods.row, tm)
    return pl.pallas_call(
        functools.partial(_ffn_kernel, nj=nj),
        out_shape=jax.ShapeDtypeStruct((n, d), F32),
        grid=(n // tm, nj),
        in_specs=[pl.BlockSpec((tm, d), lambda i, j: (i, 0)),
                  _mod_spec(d, rot, k0), _mod_spec(d, rot, k0 + 1), _mod_spec(d, rot, k0 + 2),
                  pl.BlockSpec((d, tf), lambda i, j: (0, j)),
                  pl.BlockSpec((d, tf), lambda i, j: (0, j + nj)),
                  pl.BlockSpec((tf, d), lambda i, j: (j, 0)),
                  pl.BlockSpec((1, d), lambda i, j: (0, 0)),
                  pl.BlockSpec((1, d), lambda i, j: (0, 0))],
        out_specs=pl.BlockSpec((tm, d), lambda i, j: (i, 0)),
        scratch_shapes=[pltpu.VMEM((tm, d), BF16)],
        compiler_params=_params("parallel", "arbitrary"),
        name="ffn",
    )(x, mod_rows, mod_rows, mod_rows, wi, wi, wo, lg.reshape(1, d), lb.reshape(1, d))


def _proj_kernel(x_ref, sh_ref, sc_ref, w_ref, o_ref, xm_ref):
    @pl.when(pl.program_id(1) == 0)
    def _():
        xm_ref[...] = (x_ref[...] * (1.0 + sc_ref[0]) + sh_ref[0]).astype(BF16)

    o_ref[...] = _dot(xm_ref[...], w_ref[...])


def _proj_call(x, mod_rows, mods, k0, w, tn_want=1024):
    n, d = x.shape
    nout = w.shape[1]
    tn = nout if nout % tn_want else tn_want
    tm = mods.tile(n, 1024 if tn <= tn_want else 512)
    rot = functools.partial(mods.row, tm)
    return pl.pallas_call(
        _proj_kernel,
        out_shape=jax.ShapeDtypeStruct((n, nout), F32),
        grid=(n // tm, nout // tn),
        in_specs=[pl.BlockSpec((tm, d), lambda i, j: (i, 0)),
                  _mod_spec(d, rot, k0), _mod_spec(d, rot, k0 + 1),
                  pl.BlockSpec((d, tn), lambda i, j: (0, j))],
        out_specs=pl.BlockSpec((tm, tn), lambda i, j: (i, j)),
        scratch_shapes=[pltpu.VMEM((tm, d), BF16)],
        compiler_params=_params("parallel", "arbitrary"),
        name="proj",
    )(x, mod_rows, mod_rows, w)


def _pool_kernel(z_ref, w_ref, s_ref, o_ref):
    g = pl.program_id(1)
    u = z_ref[...]
    t_len = u.shape[0]
    t = lax.broadcasted_iota(jnp.int32, u.shape, 0)
    for gi, win in enumerate(POOL_WINDOWS):
        @pl.when(g == gi)
        def _(win=win):
            half = win // 2
            acc = u
            for dlt in range(-half, half):
                if dlt == 0:
                    continue
                shifted = pltpu.roll(u, (-dlt) % t_len, 0)
                ok = (t + dlt >= 0) & (t + dlt < t_len)
                acc = acc + jnp.where(ok, shifted, 0.0)
            cnt = (jnp.minimum(t + half, t_len) - jnp.maximum(t - half, 0)).astype(F32)
            pooled = acc / cnt - u
            y = _dot(pooled.astype(BF16), w_ref[0]) * s_ref[...]
            o_ref[...] = y.astype(o_ref.dtype)


def _pool_call(z, nseq, w_bf16, scale):
    n, c = z.shape
    t_len = n // nseq
    ng = len(POOL_WINDOWS)
    return pl.pallas_call(
        _pool_kernel,
        out_shape=jax.ShapeDtypeStruct((n, c), BF16),
        grid=(nseq, ng),
        in_specs=[pl.BlockSpec((t_len, POOL_GROUP), lambda b, g: (b, g)),
                  pl.BlockSpec((1, POOL_GROUP, POOL_GROUP), lambda b, g: (g, 0, 0)),
                  pl.BlockSpec((1, POOL_GROUP), lambda b, g: (0, g))],
        out_specs=pl.BlockSpec((t_len, POOL_GROUP), lambda b, g: (b, g)),
        compiler_params=_params("parallel", "arbitrary"),
        name="pool",
    )(z, w_bf16, scale.reshape(1, c))


def _rwkv_prep_kernel(z_ref, zp_ref, zn_ref, mu_ref, w0_ref, w2_ref, a0_ref, a2_ref, g2_ref,
                      kkp_ref, ka_ref, bd_ref,
                      r_ref, v_ref, kk_ref, g_ref, k_ref, b_ref, lw_ref, *, tiles_per_seq):
    i = pl.program_id(0)
    cur = z_ref[...]
    tm = cur.shape[0]
    c = BRANCH_W
    row = lax.broadcasted_iota(jnp.int32, cur.shape, 0)
    pos = i % tiles_per_seq
    prev_edge = jnp.where(pos == 0, 0.0, zp_ref[7:8, :])
    next_edge = jnp.where(pos == tiles_per_seq - 1, 0.0, zn_ref[0:1, :])
    prev = jnp.where(row == 0, prev_edge, pltpu.roll(cur, 1, 0))
    nxt = jnp.where(row == tm - 1, next_edge, pltpu.roll(cur, tm - 1, 0))
    zs = cur + (0.5 * (prev + nxt) - cur) * mu_ref[...]

    r = zs[:, :c]
    k = zs[:, c:2 * c]
    v = zs[:, 2 * c:3 * c]
    o = 3 * c
    wd = zs[:, o:o + 2 * LORA_PAD]
    ad = zs[:, o + 2 * LORA_PAD:o + 4 * LORA_PAD]
    gd = zs[:, o + 4 * LORA_PAD:]
    z_w = w0_ref[...] + _dot(jnp.tanh(wd).astype(BF16), w2_ref[...])
    lw = -DECAY_SCALE * jax.nn.sigmoid(z_w)
    a = jax.nn.sigmoid(a0_ref[...] + _dot(ad.astype(BF16), a2_ref[...]))
    g = _dot(jax.nn.sigmoid(gd).astype(BF16), g2_ref[...])
    kk = k * kkp_ref[...]
    ssq = _dot_exact_rhs(kk * kk, bd_ref[...])
    kk = kk * lax.rsqrt(ssq + 1e-12)
    r_ref[...] = r.astype(r_ref.dtype)
    v_ref[...] = v.astype(v_ref.dtype)
    kk_ref[...] = kk.astype(kk_ref.dtype)
    g_ref[...] = g.astype(g_ref.dtype)
    for dr in range(2):
        a_d = a[:, dr * c:(dr + 1) * c]
        k_ref[dr] = (k * (1.0 + (a_d - 1.0) * ka_ref[...])).astype(k_ref.dtype)
        b_ref[dr] = (kk * a_d).astype(b_ref.dtype)
        lw_ref[dr] = lw[:, dr * c:(dr + 1) * c]


def _rwkv_prep_call(z, t_len, pw, tm_want=256):
    n, w = z.shape
    c = BRANCH_W
    tm = _row_tile(t_len, tm_want)
    tps = t_len // tm
    nblk8 = n // 8
    full = lambda shape: pl.BlockSpec(shape, lambda i: (0,) * len(shape))
    tile = pl.BlockSpec((tm, c), lambda i: (i, 0))
    tile2 = pl.BlockSpec((2, tm, c), lambda i: (0, i, 0))
    one = jax.ShapeDtypeStruct((n, c), BF16)
    two = jax.ShapeDtypeStruct((2, n, c), BF16)
    logw = jax.ShapeDtypeStruct((2, n, c), F32)
    return pl.pallas_call(
        functools.partial(_rwkv_prep_kernel, tiles_per_seq=tps),
        out_shape=(one, one, one, one, two, two, logw),
        grid=(n // tm,),
        in_specs=[pl.BlockSpec((tm, w), lambda i: (i, 0)),
                  pl.BlockSpec((8, w), lambda i: (jnp.maximum(i * (tm // 8) - 1, 0), 0)),
                  pl.BlockSpec((8, w), lambda i: (jnp.minimum((i + 1) * (tm // 8), nblk8 - 1), 0)),
                  full((1, w)), full((1, 2 * c)), full((2 * LORA_PAD, 2 * c)),
                  full((1, 2 * c)), full((2 * LORA_PAD, 2 * c)), full((GATE_LORA, c)),
                  full((1, c)), full((1, c)), full((c, c))],
        out_specs=(tile, tile, tile, tile, tile2, tile2, tile2),
        compiler_params=_params("parallel"),
        name="rwkv_prep",
    )(z, z, z, pw["mu"], pw["w0"], pw["w2"], pw["a0"], pw["a2"], pw["g2"],
      pw["kk"], pw["ka"], pw["bd"])


def _block_diag4(x_bf16, head_mask):
    return jnp.concatenate([x_bf16] * 4, axis=0) * head_mask


def _dot_bd(a, x, head_mask):
    return _dot(a.astype(BF16), _block_diag4(x.astype(BF16), head_mask))


def _scan_masks(rev, cl):
    hw = RWKV_HEAD
    gw = 4 * hw
    row = lax.broadcasted_iota(jnp.int32, (cl, gw), 0)
    lane = lax.broadcasted_iota(jnp.int32, (cl, gw), 1)
    pos_t = (cl - 1 - row) if rev else row
    pos_s = lane % hw
    pos_s = (cl - 1 - pos_s) if rev else pos_s
    nlev = cl.bit_length() - 1
    brow = lax.broadcasted_iota(jnp.int32, (gw, gw), 0) // hw
    blane = lax.broadcasted_iota(jnp.int32, (gw, gw), 1) // hw
    ti = lax.broadcasted_iota(jnp.int32, (cl, cl), 0)
    si = lax.broadcasted_iota(jnp.int32, (cl, cl), 1)
    return {
        "lane_head": lane // hw,
        "strict": pos_s < pos_t,
        "incl": pos_s <= pos_t,
        "eye": jnp.where(pos_s == pos_t, 1.0, 0.0),
        "levels": [((pos_t >> (lev + 1)) == (pos_s >> (lev + 1)))
                   & (((pos_t >> lev) & 1) == 1) & (((pos_s >> lev) & 1) == 0)
                   for lev in range(nlev)],
        "head": jnp.where(brow == blane, 1.0, 0.0).astype(BF16),
        "tri": jnp.where((si >= ti) if rev else (si <= ti), 1.0, 0.0).astype(BF16),
    }


def _scan_group(mk, ab, qb, kb, bb, kh, bh, vb, g_tot, s_ref, sl, out):
    cl = ab.shape[0]
    hw = RWKV_HEAD
    hm = mk["head"]
    aq = jnp.concatenate([ab, qb], axis=0)
    p_b = _dot(aq, _block_diag4(bb, hm), _NT)
    p_k = _dot(aq, _block_diag4(kb, hm), _NT)
    n_ab = jnp.where(mk["strict"], p_b[:cl], 0.0)
    a_ak = jnp.where(mk["strict"], p_k[:cl], 0.0).astype(BF16)
    a_qb = jnp.where(mk["incl"], p_b[cl:], 0.0).astype(BF16)
    a_qk = jnp.where(mk["incl"], p_k[cl:], 0.0).astype(BF16)
    yield
    t_inv = mk["eye"] - jnp.where(mk["levels"][0], n_ab, 0.0)
    for lev in range(1, len(mk["levels"])):
        x = _dot_bd(jnp.where(mk["levels"][lev], n_ab, 0.0), t_inv, hm)
        yield
        t_inv = t_inv - _dot_bd(t_inv, x, hm)
        yield
    s_g = s_ref[:, sl]
    p_s = _dot(aq, _block_diag4(s_g.astype(BF16), hm), _NT)
    p_v = _dot(jnp.concatenate([a_ak, a_qk], axis=0), _block_diag4(vb, hm))
    yield
    ub = (-_dot_bd(t_inv, p_s[:cl] + p_v[:cl], hm)).astype(BF16)
    yield
    out.append(p_s[cl:] + p_v[cl:] + _dot(a_qb, _block_diag4(ub, hm)))
    full = _dot(jnp.concatenate([ub, vb], axis=0), jnp.concatenate([bh, kh], axis=0), _TN)
    upd = jnp.where(mk["lane_head"] == 0, full[:hw], 0.0)
    for h in range(1, 4):
        upd = upd + jnp.where(mk["lane_head"] == h, full[h * hw:(h + 1) * hw], 0.0)
    s_ref[:, sl] = s_g * g_tot + upd


def _scan_direction(mk, r, v, kk, k, b, lw, s_ref, out):
    cl, c = lw.shape
    gw = 4 * RWKV_HEAD
    assert cl == RWKV_HEAD
    cum = _dot_exact_lhs(mk["tri"], lw)
    tot = jnp.sum(lw, axis=0, keepdims=True)
    g_inv = jnp.exp(-cum)
    g_end = jnp.exp(tot - cum)
    g_tot = jnp.exp(tot)
    qb = (r * jnp.exp(cum)).astype(BF16)
    ab = (kk * jnp.exp(cum - lw)).astype(BF16)
    kb = (k * g_inv).astype(BF16)
    bb = (b * g_inv).astype(BF16)
    kh = (k * g_end).astype(BF16)
    bh = (b * g_end).astype(BF16)
    vb = v.astype(BF16)
    gens = []
    for g in range(c // gw):
        sl = slice(g * gw, (g + 1) * gw)
        gens.append(_scan_group(mk, ab[:, sl], qb[:, sl], kb[:, sl], bb[:, sl], kh[:, sl], bh[:, sl],
                                vb[:, sl], g_tot[:, sl], s_ref, sl, out))
    return gens


SCAN_ROWS_PER_STEP = 4


def _scan_kernel(rf_ref, rb_ref, vf_ref, vb_ref, kkf_ref, kkb_ref, kf_ref, kb_ref, bf_ref, bb_ref,
                 lwf_ref, lwb_ref, s0_ref, of_ref, ob_ref, sfin_ref, s_ref):
    n = pl.program_id(1)

    @pl.when(n == 0)
    def _():
        s_ref[...] = s0_ref[...]

    nrow, cl, _ = rf_ref.shape
    mk_f = _scan_masks(False, cl)
    mk_b = _scan_masks(True, cl)
    gens, outs = [], []
    for i in range(nrow):
        out_f, out_b = [], []
        outs.append((out_f, out_b))
        gens += _scan_direction(mk_f, rf_ref[i], vf_ref[i], kkf_ref[i], kf_ref[i], bf_ref[i],
                                lwf_ref[i], s_ref.at[0, i], out_f)
        gens += _scan_direction(mk_b, rb_ref[i], vb_ref[i], kkb_ref[i], kb_ref[i], bb_ref[i],
                                lwb_ref[i], s_ref.at[1, i], out_b)
    _run_chains(gens)
    for i, (out_f, out_b) in enumerate(outs):
        of_ref[i] = jnp.concatenate(out_f, axis=1)
        ob_ref[i] = jnp.concatenate(out_b, axis=1)

    @pl.when(n == pl.num_programs(1) - 1)
    def _():
        sfin_ref[...] = s_ref[...]


def _scan_call(prep, s0, nbatch, t_len):
    r, v, kk, _, k2, b2, lw2 = prep
    n, c = r.shape
    cl = SCAN_CHUNK
    nc = t_len // cl
    nrow = math.gcd(nbatch, SCAN_ROWS_PER_STEP)
    seq = lambda a: a.reshape(a.shape[:-2] + (nbatch, t_len, c))
    fwd = pl.BlockSpec((nrow, cl, c), lambda bi, i: (bi, i, 0))
    bwd = pl.BlockSpec((nrow, cl, c), lambda bi, i: (bi, nc - 1 - i, 0))
    fwd2 = pl.BlockSpec((None, nrow, cl, c), lambda bi, i: (0, bi, i, 0))
    bwd2 = pl.BlockSpec((None, nrow, cl, c), lambda bi, i: (1, bi, nc - 1 - i, 0))
    state = pl.BlockSpec((2, nrow, RWKV_HEAD, c), lambda bi, i: (0, bi, 0, 0))
    o_f, o_b, s_fin = pl.pallas_call(
        _scan_kernel,
        out_shape=(jax.ShapeDtypeStruct((nbatch, t_len, c), F32),
                   jax.ShapeDtypeStruct((nbatch, t_len, c), F32),
                   jax.ShapeDtypeStruct(s0.shape, F32)),
        grid=(nbatch // nrow, nc),
        in_specs=[fwd, bwd, fwd, bwd, fwd, bwd, fwd2, bwd2, fwd2, bwd2, fwd2, bwd2, state],
        out_specs=(fwd, bwd, state),
        scratch_shapes=[pltpu.VMEM((2, nrow, RWKV_HEAD, c), F32)],
        compiler_params=_params("parallel", "arbitrary"),
        name="rwkv_scan",
    )(seq(r), seq(r), seq(v), seq(v), seq(kk), seq(kk), seq(k2), seq(k2), seq(b2), seq(b2),
      seq(lw2), seq(lw2), s0)
    return o_f.reshape(n, c), o_b.reshape(n, c), s_fin


def _rwkv_out_kernel(of_ref, ob_ref, r_ref, k_ref, v_ref, g_ref, rk_ref, gg_ref, gb_ref, bd_ref, y_ref):
    bd = bd_ref[...]
    inv_n = 1.0 / RWKV_HEAD
    s = of_ref[...] + ob_ref[...]
    mu = _dot_exact_rhs(s, bd) * inv_n
    sc = s - mu
    var = _dot_exact_rhs(sc * sc, bd) * inv_n
    y = sc * lax.rsqrt(var + GN_EPS) * gg_ref[...] + gb_ref[...]
    r = r_ref[...].astype(F32)
    k_sum = k_ref[0].astype(F32) + k_ref[1].astype(F32)
    bonus = _dot_exact_rhs(r * k_sum * rk_ref[...], bd)
    y_ref[...] = ((y + bonus * v_ref[...].astype(F32)) * g_ref[...].astype(F32)).astype(y_ref.dtype)


def _rwkv_out_call(o_f, o_b, prep, pw, tm_want=512):
    r, v, _, g, k2, _, _ = prep
    n, c = r.shape
    tm = _row_tile(n, tm_want)
    tile = pl.BlockSpec((tm, c), lambda i: (i, 0))
    tile2 = pl.BlockSpec((2, tm, c), lambda i: (0, i, 0))
    vec = pl.BlockSpec((1, c), lambda i: (0, 0))
    return pl.pallas_call(
        _rwkv_out_kernel,
        out_shape=jax.ShapeDtypeStruct((n, c), BF16),
        grid=(n // tm,),
        in_specs=[tile, tile, tile, tile2, tile, tile, vec, vec, vec,
                  pl.BlockSpec((c, c), lambda i: (0, 0))],
        out_specs=tile,
        compiler_params=_params("parallel"),
        name="rwkv_out",
    )(o_f, o_b, r, k2, v, g, pw["rk"], pw["gn_g"], pw["gn_b"], pw["bd"])


def _attn_prep_kernel(zw_ref, zd_ref, cos_ref, sin_ref, qn_ref, kn_ref,
                      qw_ref, kw_ref, vw_ref, qd_ref, kd_ref, vd_ref, *, rope):
    scale = HEAD_DIM ** -0.5
    hd = HEAD_DIM
    if rope:
        cosf = cos_ref[...]
        sins = sin_ref[...]
        lane = lax.broadcasted_iota(jnp.int32, cosf.shape, 1)
        first = (lane % (hd // 2)) < (hd // 4)

    def rot(xh):
        if not rope:
            return xh
        partner = jnp.where(first, pltpu.roll(xh, hd - hd // 4, 1), pltpu.roll(xh, hd // 4, 1))
        return xh * cosf + partner * sins

    def rms(xh, gain):
        return xh * lax.rsqrt(jnp.mean(xh * xh, axis=-1, keepdims=True) + RMS_EPS) * gain

    for h in range(Q_HEADS):
        sl = slice(h * hd, (h + 1) * hd)
        qw_ref[:, sl] = (rot(zw_ref[:, sl]) * scale).astype(BF16)
        qd_ref[:, sl] = (rot(rms(zd_ref[:, sl], qn_ref[...])) * scale).astype(BF16)
    for h in range(KV_HEADS):
        sl = slice(h * hd, (h + 1) * hd)
        ks = slice((Q_HEADS + h) * hd, (Q_HEADS + h + 1) * hd)
        vs = slice((Q_HEADS + KV_HEADS + h) * hd, (Q_HEADS + KV_HEADS + h + 1) * hd)
        kw_ref[:, sl] = rot(zw_ref[:, ks]).astype(BF16)
        kd_ref[:, sl] = rot(rms(zd_ref[:, ks], kn_ref[...])).astype(BF16)
        vw_ref[:, sl] = zw_ref[:, vs].astype(BF16)
        vd_ref[:, sl] = zd_ref[:, vs].astype(BF16)


def _attn_prep_call(zw, zd, t_len, cosf, sins, qn, kn, rope, tm_want=256):
    n = zw.shape[0]
    tm = _row_tile(t_len, tm_want)
    tps = t_len // tm
    hd = HEAD_DIM
    tile = lambda w: pl.BlockSpec((tm, w), lambda i: (i, 0))
    pos = pl.BlockSpec((tm, hd), lambda i: (i % tps, 0))
    vec = pl.BlockSpec((1, hd), lambda i: (0, 0))
    qs = jax.ShapeDtypeStruct((n, Q_HEADS * hd), BF16)
    ks = jax.ShapeDtypeStruct((n, KV_HEADS * hd), BF16)
    return pl.pallas_call(
        functools.partial(_attn_prep_kernel, rope=rope),
        out_shape=(qs, ks, ks, qs, ks, ks),
        grid=(n // tm,),
        in_specs=[tile(ATT_W), tile(ATT_W), pos, pos, vec, vec],
        out_specs=(tile(Q_HEADS * hd), tile(KV_HEADS * hd), tile(KV_HEADS * hd),
                   tile(Q_HEADS * hd), tile(KV_HEADS * hd), tile(KV_HEADS * hd)),
        compiler_params=_params("parallel"),
        name="attn_prep",
    )(zw, zd, cosf, sins, qn.reshape(1, hd), kn.reshape(1, hd))


def _full_attn_kernel(q_ref, k_ref, v_ref, *rest, sink):
    if sink:
        sink_ref, o_ref = rest
    else:
        (o_ref,) = rest
    hd = HEAD_DIM
    k = k_ref[...]
    v = v_ref[...]
    group = Q_HEADS // KV_HEADS
    tq = q_ref.shape[0]
    rows = min(tq, FULL_ATTN_CHAIN_ROWS)

    def chain(g, r0):
        sl = slice(g * hd, (g + 1) * hd)
        rs = slice(r0, r0 + rows)
        s = _dot(q_ref[rs, sl], k, _NT)
        yield
        m = jnp.max(s, axis=-1, keepdims=True)
        if sink:
            sk = sink_ref[:, g * hd:g * hd + 1]
            m = jnp.maximum(m, sk)
        p = jnp.exp(s - m)
        den = jnp.sum(p, axis=-1, keepdims=True)
        if sink:
            den = den + jnp.exp(sk - m)
        yield
        o_ref[rs, sl] = (_dot(p.astype(BF16), v) / den).astype(o_ref.dtype)

    _run_chains([chain(g, r0) for g in range(group) for r0 in range(0, tq, rows)], 1)


FULL_ATTN_CHAIN_ROWS = 128


def _full_attn_call(q, k, v, nbatch, sink_row=None, tq_want=512):
    n = q.shape[0]
    hd = HEAD_DIM
    tq_len = n // nbatch
    tk_len = k.shape[0] // nbatch
    tq = _row_tile(tq_len, tq_want)
    nq = tq_len // tq
    gw = (Q_HEADS // KV_HEADS) * hd
    in_specs = [pl.BlockSpec((tq, gw), lambda b, h, i: (b * nq + i, h)),
                pl.BlockSpec((tk_len, hd), lambda b, h, i: (b, h)),
                pl.BlockSpec((tk_len, hd), lambda b, h, i: (b, h))]
    args = [q, k, v]
    if sink_row is not None:
        in_specs.append(pl.BlockSpec((1, gw), lambda b, h, i: (0, h)))
        args.append(sink_row)
    return pl.pallas_call(
        functools.partial(_full_attn_kernel, sink=sink_row is not None),
        out_shape=jax.ShapeDtypeStruct((n, Q_HEADS * hd), BF16),
        grid=(nbatch, KV_HEADS, nq),
        in_specs=in_specs,
        out_specs=pl.BlockSpec((tq, gw), lambda b, h, i: (b * nq + i, h)),
        compiler_params=_params("parallel", "parallel", "arbitrary"),
        name="full_attn",
    )(*args)


WIN_BLOCKS_PER_STEP = 8


def _win_attn_kernel(q_ref, kx_ref, vx_ref, kc_ref, vc_ref, sink_ref, o_ref):
    hd = HEAD_DIM
    step = pl.program_id(2)
    t_len = kx_ref.shape[0]
    nsub = q_ref.shape[0] // BLOCK
    span = 3 * BLOCK
    kc = kc_ref[...]
    vc = vc_ref[...]
    l_len = kc.shape[0]
    group = Q_HEADS // KV_HEADS

    def block_keys(j):
        nblk = step * nsub + j
        start = pl.multiple_of(jnp.clip((nblk - 1) * BLOCK, 0, t_len - span), BLOCK)
        k_all = jnp.concatenate([kc, kx_ref[pl.ds(start, span), :]], axis=0)
        v_all = jnp.concatenate([vc, vx_ref[pl.ds(start, span), :]], axis=0)
        col = lax.broadcasted_iota(jnp.int32, (BLOCK, l_len + span), 1)
        qpos = nblk * BLOCK + lax.broadcasted_iota(jnp.int32, (BLOCK, l_len + span), 0)
        allowed = (col < l_len) | (jnp.abs(qpos - (start + col - l_len)) <= WIN)
        return k_all, v_all, allowed

    def chain(j, g, keys):
        k_all, v_all, allowed = keys
        rs = slice(j * BLOCK, (j + 1) * BLOCK)
        sl = slice(g * hd, (g + 1) * hd)
        s = _dot(q_ref[rs, sl], k_all, _NT)
        yield
        s = jnp.where(allowed, s, NEG_INF)
        sk = sink_ref[:, g * hd:g * hd + 1]
        m = jnp.maximum(jnp.max(s, axis=-1, keepdims=True), sk)
        p = jnp.exp(s - m)
        den = jnp.sum(p, axis=-1, keepdims=True) + jnp.exp(sk - m)
        yield
        o_ref[rs, sl] = (_dot(p.astype(BF16), v_all) / den).astype(o_ref.dtype)

    chains = []
    for j in range(nsub):
        keys = block_keys(j)
        chains += [chain(j, g, keys) for g in range(group)]
    _run_chains(chains, 4)


def _win_attn_call(q, kx, vx, kc, vc, sink_row, nbatch):
    n = q.shape[0]
    hd = HEAD_DIM
    t_len = n // nbatch
    l_len = kc.shape[0] // nbatch
    tq = _row_tile(t_len, WIN_BLOCKS_PER_STEP * BLOCK)
    nq = t_len // tq
    gw = (Q_HEADS // KV_HEADS) * hd
    return pl.pallas_call(
        _win_attn_kernel,
        out_shape=jax.ShapeDtypeStruct((n, Q_HEADS * hd), BF16),
        grid=(nbatch, KV_HEADS, nq),
        in_specs=[pl.BlockSpec((tq, gw), lambda b, h, i: (b * nq + i, h)),
                  pl.BlockSpec((t_len, hd), lambda b, h, i: (b, h)),
                  pl.BlockSpec((t_len, hd), lambda b, h, i: (b, h)),
                  pl.BlockSpec((l_len, hd), lambda b, h, i: (b, h)),
                  pl.BlockSpec((l_len, hd), lambda b, h, i: (b, h)),
                  pl.BlockSpec((1, gw), lambda b, h, i: (0, h))],
        out_specs=pl.BlockSpec((tq, gw), lambda b, h, i: (b * nq + i, h)),
        compiler_params=_params("parallel", "parallel", "arbitrary"),
        name="win_attn",
    )(q, kx, vx, kc, vc, sink_row)


MERGE_COL_SPLIT = 2


def _merge_kernel(x_ref, sh_ref, sc_ref, gt_ref, b0_ref, b1_ref, b2_ref, b3_ref, wg_ref, bg_ref,
                  wup_ref, wout_ref, lg_ref, lb_ref, o_ref, hm_ref, acc_ref):
    kb = pl.program_id(1)
    tn = wg_ref.shape[1]

    @pl.when(kb == 0)
    def _():
        hm_ref[...] = (x_ref[...] * (1.0 + sc_ref[0]) + sh_ref[0]).astype(BF16)

    for i, br in enumerate((b0_ref, b1_ref, b2_ref, b3_ref)):
        for h in range(MERGE_COL_SPLIT):
            @pl.when(kb == i * MERGE_COL_SPLIT + h)
            def _(i=i, br=br, h=h):
                cols = slice(h * tn, (h + 1) * tn)
                gate = jax.nn.sigmoid(_dot(hm_ref[...], wg_ref[...]) + bg_ref[...])
                y = gate * _dot(br[...], wup_ref[0])
                if i == 0:
                    acc_ref[:, cols] = y
                else:
                    acc_ref[:, cols] += y

    @pl.when(kb == pl.num_programs(1) - 1)
    def _():
        out = _dot(acc_ref[...].astype(BF16), wout_ref[...])
        y = DEEPNORM_ALPHA * x_ref[...] + gt_ref[0] * out
        o_ref[...] = _layer_norm(y, lg_ref[...], lb_ref[...])


def _merge_call(x, mod_rows, mods, branches, w_gate, b_gate, wup, wout, lg, lb,
                tm_want=512):
    n, d = x.shape
    c = BRANCH_W
    tm = mods.tile(n, tm_want)
    tn = d // MERGE_COL_SPLIT
    rot = functools.partial(mods.row, tm)
    br_spec = pl.BlockSpec((tm, c), lambda i, j: (i, 0))
    return pl.pallas_call(
        _merge_kernel,
        out_shape=jax.ShapeDtypeStruct((n, d), F32),
        grid=(n // tm, N_BRANCH * MERGE_COL_SPLIT),
        in_specs=[pl.BlockSpec((tm, d), lambda i, j: (i, 0)),
                  _mod_spec(d, rot, 3), _mod_spec(d, rot, 4), _mod_spec(d, rot, 5),
                  br_spec, br_spec, br_spec, br_spec,
                  pl.BlockSpec((d, tn), lambda i, j: (0, j)),
                  pl.BlockSpec((1, tn), lambda i, j: (0, j)),
                  pl.BlockSpec((1, c, tn), lambda i, j: (j // MERGE_COL_SPLIT, 0, j % MERGE_COL_SPLIT)),
                  pl.BlockSpec((d, d), lambda i, j: (0, 0), pipeline_mode=pl.Buffered(1)),
                  pl.BlockSpec((1, d), lambda i, j: (0, 0)),
                  pl.BlockSpec((1, d), lambda i, j: (0, 0))],
        out_specs=pl.BlockSpec((tm, d), lambda i, j: (i, 0)),
        scratch_shapes=[pltpu.VMEM((tm, d), BF16), pltpu.VMEM((tm, d), F32)],
        compiler_params=_params("parallel", "arbitrary"),
        name="merge",
    )(x, mod_rows, mod_rows, mod_rows, *branches, w_gate, b_gate.reshape(1, N_BRANCH * d), wup, wout,
      lg.reshape(1, d), lb.reshape(1, d))


def _pad_rows(w, rows):
    return jnp.pad(w, ((0, rows - w.shape[0]), (0, 0)))


def _rwkv_weights(w_rwkv, mu, w0, w2, a0, a2, g2, kk, ka, rk, gn_g, gn_b):
    c = BRANCH_W
    o = 3 * c
    cuts = [o, o + DECAY_LORA, o + 2 * DECAY_LORA, o + 2 * DECAY_LORA + ICLR_LORA,
            o + 2 * DECAY_LORA + 2 * ICLR_LORA]

    def pad_cols(m):
        parts = [m[:, :o]]
        for lo, hi in zip(cuts[:-1], cuts[1:]):
            parts.append(jnp.pad(m[:, lo:hi], ((0, 0), (0, LORA_PAD - (hi - lo)))))
        parts.append(m[:, cuts[-1]:])
        return jnp.concatenate(parts, axis=1)

    def block_diag(m):
        z = jnp.zeros((LORA_PAD, c), m.dtype)
        top = jnp.concatenate([_pad_rows(m[0], LORA_PAD), z], axis=1)
        bot = jnp.concatenate([z, _pad_rows(m[1], LORA_PAD)], axis=1)
        return jnp.concatenate([top, bot], axis=0)

    head = jnp.arange(c) // RWKV_HEAD
    return {
        "w": pad_cols(w_rwkv).astype(BF16),
        "mu": pad_cols(mu.reshape(1, -1)),
        "w0": w0.reshape(1, 2 * c), "w2": block_diag(w2).astype(BF16),
        "a0": a0.reshape(1, 2 * c), "a2": block_diag(a2).astype(BF16),
        "g2": g2.astype(BF16),
        "kk": kk.reshape(1, c), "ka": ka.reshape(1, c), "rk": rk.reshape(1, c),
        "gn_g": gn_g.reshape(1, c), "gn_b": gn_b.reshape(1, c),
        "bd": (head[:, None] == head[None, :]).astype(BF16),
    }


def _rope_tables(t_len):
    rows = t_len // GRID_W
    row = jnp.repeat(jnp.arange(rows), GRID_W).astype(F32)
    col = (jnp.arange(t_len) % GRID_W).astype(F32)
    n_freq = HEAD_DIM // 4
    inv = ROPE_THETA ** (-jnp.arange(n_freq, dtype=F32) / n_freq)
    ar = row[:, None] * inv
    ac = col[:, None] * inv
    cosf = jnp.concatenate([jnp.cos(ar), jnp.cos(ar), jnp.cos(ac), jnp.cos(ac)], axis=1)
    sins = jnp.concatenate([-jnp.sin(ar), jnp.sin(ar), -jnp.sin(ac), jnp.sin(ac)], axis=1)
    return cosf, sins


def kernel(x, c, ctx, c_ctx, w_mod, b_mod, ln_g, ln_b, ffn1_wi, ffn1_wo, ffn2_wi, ffn2_wo, w_in, b_gate, pool_w, pool_scale, rwkv_mu, rwkv_w0, rwkv_w2, rwkv_a0, rwkv_a2, rwkv_g2, rwkv_kk, rwkv_ka, rwkv_rk, rwkv_gn_g, rwkv_gn_b, c_sink, d_qnorm, d_knorm, w_up, w_out):
    nb, t_len, d = x.shape
    l_len = ctx.shape[1]
    depth = w_mod.shape[0]
    cw = BRANCH_W
    assert depth == DEPTH and t_len % GRID_W == 0 and t_len >= 3 * BLOCK

    mod_pad = 16
    c_all = jnp.zeros((mod_pad, d), F32).at[:nb].set(c).at[nb].set(c_ctx)
    cosf, sins = _rope_tables(t_len)
    s0 = jnp.zeros((2, nb, RWKV_HEAD, BRANCH_W), F32)

    row_x = _ModIndex(0, t_len)
    row_c = _ModIndex(nb, nb * l_len)

    xs = x.reshape(nb * t_len, d)
    xc = ctx.reshape(nb * l_len, d)
    o1 = cw
    o2 = o1 + 3 * cw + 2 * DECAY_LORA + 2 * ICLR_LORA + GATE_LORA
    o3 = o2 + ATT_W
    o4 = o3 + ATT_W

    for l in range(depth):
        last = l == depth - 1
        mod_rows = _mod_call(c_all, w_mod[l], b_mod[l]).reshape(mod_pad * N_MOD, 1, d)
        wi1, wo1 = ffn1_wi[l].astype(BF16), ffn1_wo[l].astype(BF16)
        wi2, wo2 = ffn2_wi[l].astype(BF16), ffn2_wo[l].astype(BF16)
        w_pool = w_in[l][:, :o1].astype(BF16)
        pw = _rwkv_weights(w_in[l][:, o1:o2], rwkv_mu[l], rwkv_w0[l], rwkv_w2[l], rwkv_a0[l],
                           rwkv_a2[l], rwkv_g2[l], rwkv_kk[l], rwkv_ka[l], rwkv_rk[l],
                           rwkv_gn_g[l], rwkv_gn_b[l])
        w_win = w_in[l][:, o2:o3].astype(BF16)
        w_dense = w_in[l][:, o3:o4].astype(BF16)
        w_gate = w_in[l][:, o4:].astype(BF16)
        pool_wb = pool_w[l].astype(BF16)
        wup = w_up[l].astype(BF16)
        wout = w_out[l].astype(BF16)
        sink_row = jnp.repeat(c_sink[l], HEAD_DIM).reshape(1, Q_HEADS * HEAD_DIM)

        xs = _ffn_call(xs, mod_rows, row_x, 0, wi1, wo1, ln_g[l, 0], ln_b[l, 0])
        xc = _ffn_call(xc, mod_rows, row_c, 0, wi1, wo1, ln_g[l, 0], ln_b[l, 0])

        proj_x = lambda w: _proj_call(xs, mod_rows, row_x, 3, w)
        proj_c = lambda w: _proj_call(xc, mod_rows, row_c, 3, w)

        prep_c = _rwkv_prep_call(proj_c(pw["w"]), l_len, pw)
        prep_x = _rwkv_prep_call(proj_x(pw["w"]), t_len, pw)
        oc_f, oc_b, s_ctx = _scan_call(prep_c, s0, nb, l_len)
        ox_f, ox_b, _ = _scan_call(prep_x, s_ctx, nb, t_len)
        yb_x = _rwkv_out_call(ox_f, ox_b, prep_x, pw)

        qw_c, kw_c, vw_c, qd_c, kd_c, vd_c = _attn_prep_call(
            proj_c(w_win), proj_c(w_dense), l_len, cosf, sins, d_qnorm[l], d_knorm[l], rope=False)
        qw_x, kw_x, vw_x, qd_x, kd_x, vd_x = _attn_prep_call(
            proj_x(w_win), proj_x(w_dense), t_len, cosf, sins, d_qnorm[l], d_knorm[l], rope=True)
        yc_x = _win_attn_call(qw_x, kw_x, vw_x, kw_c, vw_c, sink_row, nb)
        kv_w = KV_HEADS * HEAD_DIM
        kd_all = jnp.concatenate([kd_c.reshape(nb, l_len, kv_w), kd_x.reshape(nb, t_len, kv_w)],
                                 axis=1).reshape(nb * (l_len + t_len), kv_w)
        vd_all = jnp.concatenate([vd_c.reshape(nb, l_len, kv_w), vd_x.reshape(nb, t_len, kv_w)],
                                 axis=1).reshape(nb * (l_len + t_len), kv_w)
        yd_x = _full_attn_call(qd_x, kd_all, vd_all, nb)

        ya_x = _pool_call(proj_x(w_pool), nb, pool_wb, pool_scale[l])

        xs = _merge_call(xs, mod_rows, row_x, (ya_x, yb_x, yc_x, yd_x), w_gate, b_gate[l], wup, wout,
                         ln_g[l, 1], ln_b[l, 1])
        xs = _ffn_call(xs, mod_rows, row_x, 6, wi2, wo2, ln_g[l, 2], ln_b[l, 2])

        if not last:
            ya_c = _pool_call(proj_c(w_pool), nb, pool_wb, pool_scale[l])
            yb_c = _rwkv_out_call(oc_f, oc_b, prep_c, pw)
            yc_c = _full_attn_call(qw_c, kw_c, vw_c, nb, sink_row)
            yd_c = _full_attn_call(qd_c, kd_c, vd_c, nb)
            xc = _merge_call(xc, mod_rows, row_c, (ya_c, yb_c, yc_c, yd_c), w_gate, b_gate[l], wup, wout,
                             ln_g[l, 1], ln_b[l, 1])
            xc = _ffn_call(xc, mod_rows, row_c, 6, wi2, wo2, ln_g[l, 2], ln_b[l, 2])

    return xs.reshape(nb, t_len, d)
```
